```python
import jax, jax.numpy as jnp
from jax import lax
import numpy as np

D_MODEL = 1024
BATCH = 8
SEQ = 8192
DEPTH = 1
DEC_BATCH = 8
DEC_SEQ = 16
PAST_LEN = 2048

CHUNK = 64
EPS = 1e-6
SSM_HEADS = 32
SSM_HEADDIM = 64
D_INNER = SSM_HEADS * SSM_HEADDIM
D_STATE = 128
SSM_GROUPS = 8
HEADS_PER_GROUP = SSM_HEADS // SSM_GROUPS
CONV_W = 4
CONV_DIM = D_INNER + 2 * SSM_GROUPS * D_STATE
GROUP_NORM_DIM = D_INNER // SSM_GROUPS
N_HEADS = 16
N_KV = 4
HEAD_DIM = 64
Q_PER_KV = N_HEADS // N_KV
WINDOW = 128
WIN_CHUNKS = WINDOW // CHUNK
ATTN_DIM = N_HEADS * HEAD_DIM
KV_DIM = N_KV * HEAD_DIM
ATTN_SCALE = HEAD_DIM ** -0.5
PEER_HEADS = 8
N_KEYS = 128
N_EXPERTS = N_KEYS * N_KEYS
PEER_TOPK = 16
D_KEY = 256
D_HALF = D_KEY // 2
PEER_BLOCK = 256
OFF_XBC = D_INNER
OFF_DT = OFF_XBC + CONV_DIM
OFF_Q = OFF_DT + SSM_HEADS
OFF_K = OFF_Q + ATTN_DIM
OFF_V = OFF_K + KV_DIM
OFF_GATE = OFF_V + KV_DIM
IN_DIM = OFF_GATE + 2 * D_MODEL

kernel_name = "hybrid_ssd_swa_peer_stream_step"


def rmsnorm(x, g):
    xf = x.astype(jnp.float32)
    xf = xf * lax.rsqrt(jnp.mean(xf * xf, axis=-1, keepdims=True) + EPS)
    return (xf * g.astype(jnp.float32)).astype(x.dtype)


def sink_softmax(s, sink):
    m = jnp.maximum(jnp.max(s, axis=-1, keepdims=True), sink)
    e = jnp.exp(s - m)
    return e / (jnp.sum(e, axis=-1, keepdims=True) + jnp.exp(sink - m))


def ssd_scan(xh, a, bm, cm, h0):
    b, T = xh.shape[:2]
    L = min(T, CHUNK)
    nc = T // L

    def chunks(t):
        return jnp.moveaxis(t.reshape((b, nc, L) + t.shape[2:]), 1, 0)

    causal = jnp.tril(jnp.ones((L, L), bool))[None, :, :, None, None]

    def step(h, inp):
        xc, ac, bc, cc = inp
        xc = xc.astype(jnp.float32)
        bc = bc.astype(jnp.float32)
        cc = cc.astype(jnp.float32)
        acum = jnp.cumsum(ac.astype(jnp.float32), axis=1)
        seg = acum[:, :, None] - acum[:, None, :]
        decay = jnp.exp(jnp.where(causal, seg, -jnp.inf))
        cb = jnp.einsum("blgn,bsgn->blsg", cc, bc)
        y = jnp.einsum("blsgr,bsgrp->blgrp", cb[..., None] * decay, xc)
        y = y + jnp.einsum("blgn,bgrpn->blgrp", cc, h) * jnp.exp(acum)[..., None]
        to_end = jnp.exp(acum[:, -1:] - acum)
        h = (jnp.exp(acum[:, -1])[..., None, None] * h
             + jnp.einsum("blgn,blgrp->bgrpn", bc, to_end[..., None] * xc))
        return h, y

    h, ys = lax.scan(step, h0, (chunks(xh), chunks(a), chunks(bm), chunks(cm)))
    y = jnp.moveaxis(ys, 0, 1).reshape(xh.shape)
    return y, h


def ssd_branch(z, xbc, dt_raw, conv_hist, h0, conv_w, conv_b, dt_bias, a_log, d_skip, norm_g):
    b, T = z.shape[:2]
    xpad = jnp.concatenate([conv_hist.astype(xbc.dtype), xbc], axis=1)
    conv = conv_b
    for j in range(CONV_W):
        conv = conv + xpad[:, j:j + T] * conv_w[j]
    xbc = jax.nn.silu(conv)
    xs, bm, cm = jnp.split(xbc, [D_INNER, D_INNER + SSM_GROUPS * D_STATE], axis=-1)
    xs = xs.reshape(b, T, SSM_GROUPS, HEADS_PER_GROUP, SSM_HEADDIM)
    bm = bm.reshape(b, T, SSM_GROUPS, D_STATE)
    cm = cm.reshape(b, T, SSM_GROUPS, D_STATE)
    dt = jax.nn.softplus(dt_raw.astype(jnp.float32) + dt_bias.astype(jnp.float32))
    dt = dt.reshape(b, T, SSM_GROUPS, HEADS_PER_GROUP)
    A = -jnp.exp(a_log.astype(jnp.float32)).reshape(SSM_GROUPS, HEADS_PER_GROUP)
    h0 = h0.astype(jnp.float32).reshape(b, SSM_GROUPS, HEADS_PER_GROUP, SSM_HEADDIM, D_STATE)
    y, h = ssd_scan(xs * dt[..., None], dt * A, bm, cm, h0)
    y = y + d_skip.astype(jnp.float32).reshape(SSM_GROUPS, HEADS_PER_GROUP)[..., None] * xs
    y = y * jax.nn.silu(z.astype(jnp.float32)).reshape(y.shape)
    y = y.reshape(b, T, SSM_GROUPS, GROUP_NORM_DIM)
    y = y * lax.rsqrt(jnp.mean(y * y, axis=-1, keepdims=True) + EPS)
    y = (y.reshape(b, T, D_INNER) * norm_g.astype(jnp.float32)).astype(z.dtype)
    return y, xpad[:, T:], h.reshape(b, SSM_HEADS, SSM_HEADDIM, D_STATE)


def banded_attention(q, k, v, sinks):
    b, T = q.shape[:2]
    nc = T // CHUNK
    qc = q.reshape(b, nc, CHUNK, N_KV, Q_PER_KV, HEAD_DIM)

    def band(t):
        tc = t.reshape(b, nc, CHUNK, N_KV, HEAD_DIM)
        tp = jnp.pad(tc, ((0, 0), (WIN_CHUNKS, 0), (0, 0), (0, 0), (0, 0)))
        return jnp.concatenate([tp[:, j:j + nc] for j in range(WIN_CHUNKS + 1)], axis=2)

    kb, vb = band(k), band(v)
    s = jnp.einsum("bcqgrd,bckgd->bcgrqk", qc, kb).astype(jnp.float32) * ATTN_SCALE
    n_band = (WIN_CHUNKS + 1) * CHUNK
    key_chunk = jnp.arange(nc)[:, None] + (jnp.arange(n_band) // CHUNK)[None, :] - WIN_CHUNKS
    valid = (key_chunk >= 0)[None, :, None, None, None, :]
    s = jnp.where(valid, s, -jnp.inf)
    sink = sinks.astype(jnp.float32).reshape(N_KV, Q_PER_KV)[None, None, :, :, None, None]
    p = sink_softmax(s, sink)
    o = jnp.einsum("bcgrqk,bckgd->bcqgrd", p.astype(vb.dtype), vb)
    return o.reshape(b, T, ATTN_DIM)


def cached_attention(q, k, v, k_hist, v_hist, sinks):
    b, T = q.shape[:2]
    kk = jnp.concatenate([k_hist.astype(k.dtype), k], axis=1)
    vv = jnp.concatenate([v_hist.astype(v.dtype), v], axis=1)
    s = jnp.einsum("bqgrd,bkgd->bgrqk", q, kk).astype(jnp.float32) * ATTN_SCALE
    sink = sinks.astype(jnp.float32).reshape(N_KV, Q_PER_KV)[None, :, :, None, None]
    p = sink_softmax(s, sink)
    o = jnp.einsum("bgrqk,bkgd->bqgrd", p.astype(vv.dtype), vv)
    return o.reshape(b, T, ATTN_DIM)


def peer_ffn(h, wq, sub_keys, u, v):
    b, T, D = h.shape
    n = b * T
    nb = -(-n // PEER_BLOCK)
    t = jnp.pad(h.reshape(n, D), ((0, nb * PEER_BLOCK - n), (0, 0))).reshape(nb, PEER_BLOCK, D)
    keys = sub_keys.astype(jnp.float32)

    def one(tb):
        q = (tb @ wq).astype(jnp.float32).reshape(PEER_BLOCK, PEER_HEADS, 2, D_HALF)
        s = jnp.einsum("thid,hikd->thik", q, keys)
        sv, si = lax.top_k(s, PEER_TOPK)
        comb = sv[:, :, 0, :, None] + sv[:, :, 1, None, :]
        comb_idx = si[:, :, 0, :, None] * N_KEYS + si[:, :, 1, None, :]
        comb = comb.reshape(PEER_BLOCK, PEER_HEADS, PEER_TOPK * PEER_TOPK)
        comb_idx = comb_idx.reshape(PEER_BLOCK, PEER_HEADS, PEER_TOPK * PEER_TOPK)
        fv, fi = lax.top_k(comb, PEER_TOPK)
        eidx = jnp.take_along_axis(comb_idx, fi, axis=-1)
        w = jax.nn.softmax(fv, axis=-1)
        act = jax.nn.gelu(jnp.einsum("thkd,td->thk", u[eidx], tb).astype(jnp.float32))
        return jnp.einsum("thk,thkd->td", (w * act).astype(tb.dtype), v[eidx])

    out = lax.map(one, t).reshape(nb * PEER_BLOCK, D)[:n]
    return out.reshape(b, T, D).astype(h.dtype)


def layer(x, c, conv_hist, ssm0, k_hist, v_hist, ada_w, ada_b, norm1_g, w_in, conv_w, conv_b,
          dt_bias, a_log, d_skip, ssm_norm_g, attn_sinks, w_branch_ssm, w_branch_attn, w_out,
          norm2_g, peer_wq, peer_keys, peer_u, peer_v):
    b, T, _ = x.shape
    mod = jax.nn.silu(c) @ ada_w + ada_b
    shift1, scale1, gate1, shift2, scale2, gate2 = [m[:, None] for m in jnp.split(mod, 6, axis=-1)]
    h = rmsnorm(x, norm1_g) * (1 + scale1) + shift1
    proj = h @ w_in
    z, xbc, dt_raw, q, k, v, gates = jnp.split(
        proj, [OFF_XBC, OFF_DT, OFF_Q, OFF_K, OFF_V, OFF_GATE], axis=-1)
    y_ssm, conv_new, ssm_new = ssd_branch(z, xbc, dt_raw, conv_hist, ssm0, conv_w, conv_b,
                                          dt_bias, a_log, d_skip, ssm_norm_g)
    q = q.reshape(b, T, N_KV, Q_PER_KV, HEAD_DIM)
    k = k.reshape(b, T, N_KV, HEAD_DIM)
    v = v.reshape(b, T, N_KV, HEAD_DIM)
    if k_hist is None:
        o = banded_attention(q, k, v, attn_sinks)
        k_new, v_new = k[:, T - WINDOW:], v[:, T - WINDOW:]
    else:
        o = cached_attention(q, k, v, k_hist, v_hist, attn_sinks)
        k_new, v_new = k, v
    g_ssm, g_attn = jnp.split(jax.nn.sigmoid(gates), 2, axis=-1)
    merged = g_ssm * (y_ssm @ w_branch_ssm) + g_attn * (o @ w_branch_attn)
    x = x + gate1 * (merged @ w_out)
    h2 = rmsnorm(x, norm2_g) * (1 + scale2) + shift2
    x = x + gate2 * peer_ffn(h2, peer_wq, peer_keys, peer_u, peer_v)
    return x, k_new, v_new, conv_new, ssm_new


def setup_inputs(seed: int = 0) -> dict:
    key = jax.random.key(seed)
    ks = jax.random.split(key, 32)
    f32 = jnp.float32

    def nrm(k, shape, scale):
        return jax.random.normal(k, shape, f32) * scale

    win_cache = min(WINDOW, PAST_LEN)
    dt0 = jnp.exp(jax.random.uniform(ks[12], (DEPTH, SSM_HEADS), f32, np.log(1e-3), np.log(1e-1)))
    return {
        "x_prompt": nrm(ks[0], (BATCH, SEQ, D_MODEL), 1.0),
        "x_sample": nrm(ks[1], (DEC_BATCH, DEC_SEQ, D_MODEL), 1.0),
        "cache_attn_k": nrm(ks[2], (DEPTH, DEC_BATCH, win_cache, N_KV, HEAD_DIM), 1.0),
        "cache_attn_v": nrm(ks[3], (DEPTH, DEC_BATCH, win_cache, N_KV, HEAD_DIM), 1.0),
        "state_conv": nrm(ks[4], (DEPTH, DEC_BATCH, CONV_W - 1, CONV_DIM), 1.0),
        "state_ssm": nrm(ks[5], (DEPTH, DEC_BATCH, SSM_HEADS, SSM_HEADDIM, D_STATE), 0.1),
        "c_prompt": nrm(ks[6], (BATCH, D_MODEL), 1.0),
        "c_sample": nrm(ks[7], (DEC_BATCH, D_MODEL), 1.0),
        "ada_w": nrm(ks[8], (DEPTH, D_MODEL, 6 * D_MODEL), 0.5 * D_MODEL ** -0.5),
        "ada_b": nrm(ks[9], (DEPTH, 6 * D_MODEL), 0.01),
        "norm1_g": 1.0 + nrm(ks[10], (DEPTH, D_MODEL), 0.05),
        "w_in": nrm(ks[11], (DEPTH, D_MODEL, IN_DIM), D_MODEL ** -0.5),
        "conv_w": nrm(ks[13], (DEPTH, CONV_W, CONV_DIM), CONV_W ** -0.5),
        "conv_b": nrm(ks[14], (DEPTH, CONV_DIM), 0.01),
        "dt_bias": dt0 + jnp.log(-jnp.expm1(-dt0)),
        "a_log": jnp.log(jax.random.uniform(ks[15], (DEPTH, SSM_HEADS), f32, 1.0, 16.0)),
        "d_skip": 1.0 + nrm(ks[16], (DEPTH, SSM_HEADS), 0.1),
        "ssm_norm_g": 1.0 + nrm(ks[17], (DEPTH, D_INNER), 0.05),
        "attn_sinks": nrm(ks[18], (DEPTH, N_HEADS), 1.0),
        "w_branch_ssm": nrm(ks[19], (DEPTH, D_INNER, D_MODEL), D_INNER ** -0.5),
        "w_branch_attn": nrm(ks[20], (DEPTH, ATTN_DIM, D_MODEL), ATTN_DIM ** -0.5),
        "w_out": nrm(ks[21], (DEPTH, D_MODEL, D_MODEL), D_MODEL ** -0.5),
        "norm2_g": 1.0 + nrm(ks[22], (DEPTH, D_MODEL), 0.05),
        "peer_wq": nrm(ks[23], (DEPTH, D_MODEL, PEER_HEADS * D_KEY), D_MODEL ** -0.5),
        "peer_keys": nrm(ks[24], (DEPTH, PEER_HEADS, 2, N_KEYS, D_HALF), D_HALF ** -0.5),
        "peer_u": nrm(ks[25], (DEPTH, N_EXPERTS, D_MODEL), D_MODEL ** -0.5),
        "peer_v": nrm(ks[26], (DEPTH, N_EXPERTS, D_MODEL), PEER_HEADS ** -0.5),
        "final_g": 1.0 + nrm(ks[27], (D_MODEL,), 0.05),
    }


def reference(x_prompt, x_sample, cache_attn_k, cache_attn_v, state_conv, state_ssm, c_prompt, c_sample,
              ada_w, ada_b, norm1_g, w_in, conv_w, conv_b, dt_bias, a_log, d_skip, ssm_norm_g,
              attn_sinks, w_branch_ssm, w_branch_attn, w_out, norm2_g, peer_wq, peer_keys,
              peer_u, peer_v, final_g):
    xp, xs = x_prompt, x_sample
    bp = xp.shape[0]
    kp_l, vp_l, cp_l, sp_l = [], [], [], []
    ks_l, vs_l, cs_l, ss_l = [], [], [], []
    for l in range(DEPTH):
        lw = (ada_w[l], ada_b[l], norm1_g[l], w_in[l], conv_w[l], conv_b[l], dt_bias[l], a_log[l],
              d_skip[l], ssm_norm_g[l], attn_sinks[l], w_branch_ssm[l], w_branch_attn[l], w_out[l],
              norm2_g[l], peer_wq[l], peer_keys[l], peer_u[l], peer_v[l])
        conv0 = jnp.zeros((bp, CONV_W - 1, CONV_DIM), xp.dtype)
        ssm0 = jnp.zeros((bp, SSM_HEADS, SSM_HEADDIM, D_STATE), jnp.float32)
        xp, kp, vp, cp, sp = layer(xp, c_prompt, conv0, ssm0, None, None, *lw)
        xs, kn, vn, cn, sn = layer(xs, c_sample, state_conv[l], state_ssm[l],
                                   cache_attn_k[l], cache_attn_v[l], *lw)
        kp_l.append(kp); vp_l.append(vp); cp_l.append(cp); sp_l.append(sp)
        ks_l.append(kn); vs_l.append(vn); cs_l.append(cn); ss_l.append(sn)
    y_prompt = rmsnorm(xp, final_g)
    y_sample = rmsnorm(xs, final_g)
    return (y_prompt, y_sample,
            jnp.stack(kp_l), jnp.stack(vp_l), jnp.stack(cp_l), jnp.stack(sp_l),
            jnp.stack(ks_l), jnp.stack(vs_l), jnp.stack(cs_l), jnp.stack(ss_l))
```

```python
import functools

import jax
import jax.numpy as jnp
from jax import lax
from jax.experimental import pallas as pl
from jax.experimental.pallas import tpu as pltpu

F32 = jnp.float32
BF16 = jnp.bfloat16

EPS = 1e-6
CHUNK = 64
WINDOW = 128
SSM_HEADS = 32
SSM_HEADDIM = 64
D_STATE = 128
SSM_GROUPS = 8
HEADS_PER_GROUP = SSM_HEADS // SSM_GROUPS
GROUP_W = HEADS_PER_GROUP * SSM_HEADDIM
D_INNER = SSM_HEADS * SSM_HEADDIM
CONV_W = 4
CONV_DIM = D_INNER + 2 * SSM_GROUPS * D_STATE
N_HEADS = 16
N_KV = 4
HEAD_DIM = 64
Q_PER_KV = N_HEADS // N_KV
ATTN_DIM = N_HEADS * HEAD_DIM
KV_DIM = N_KV * HEAD_DIM
ATTN_SCALE = HEAD_DIM ** -0.5
PEER_HEADS = 8
N_KEYS = 128
N_EXPERTS = N_KEYS * N_KEYS
PEER_TOPK = 16
D_HALF = 128

LANE = 128
SUBLANE = 8
DT_PAD = LANE

COL_XBC = 0
COL_Z = COL_XBC + CONV_DIM
COL_GATE = COL_Z + D_INNER
COL_Q = COL_GATE + 2048
COL_K = COL_Q + ATTN_DIM
COL_V = COL_K + KV_DIM
COL_DT = COL_V + KV_DIM
PROJ_W = COL_DT + DT_PAD
PROJ_TN = 1408

VMEM_LIMIT = 56 * 1024 * 1024


def _cparams(sem):
    return pltpu.CompilerParams(dimension_semantics=sem, vmem_limit_bytes=VMEM_LIMIT)


def _sigmoid(x):
    return 1.0 / (1.0 + jnp.exp(-x))


def _split2(v):
    hi = v.astype(BF16)
    lo = (v - hi.astype(F32)).astype(BF16)
    return hi, lo


def _split3(v):
    hi = v.astype(BF16)
    r = v - hi.astype(F32)
    mid = r.astype(BF16)
    lo = (r - mid.astype(F32)).astype(BF16)
    return hi, mid, lo


def _dot(a, b):
    return jnp.dot(a, b, preferred_element_type=F32)


def _dot_nt(a, b):
    return lax.dot_general(a, b, (((1,), (1,)), ((), ())), preferred_element_type=F32)


def _dot_tn(a, b):
    return lax.dot_general(a, b, (((0,), (0,)), ((), ())), preferred_element_type=F32)


def _adaln_kernel(c_ref, w_ref, b_ref, o_ref):
    c = c_ref[...]
    s = c * _sigmoid(c)
    s_hi, s_lo = _split2(s)
    w_hi, w_lo = _split2(w_ref[...])
    acc = _dot(s_hi, w_hi) + _dot(s_hi, w_lo) + _dot(s_lo, w_hi)
    o_ref[...] = acc + b_ref[...]


def _adaln(c, ada_w, ada_b):
    rows, d = c.shape
    n = ada_w.shape[1]
    tn = 1024
    return pl.pallas_call(
        _adaln_kernel,
        grid=(n // tn,),
        in_specs=[pl.BlockSpec((rows, d), lambda j: (0, 0)),
                  pl.BlockSpec((d, tn), lambda j: (0, j)),
                  pl.BlockSpec((1, tn), lambda j: (0, j))],
        out_specs=pl.BlockSpec((rows, tn), lambda j: (0, j)),
        out_shape=jax.ShapeDtypeStruct((rows, n), F32),
        compiler_params=_cparams(("arbitrary",)),
        name="adaln",
    )(c, ada_w, ada_b.reshape(1, n))


def _proj_kernel(x_ref, mod_ref, g_ref, w_ref, o_ref, h_ref):
    @pl.when(pl.program_id(2) == 0)
    def _():
        x = x_ref[...]
        ms = jnp.mean(x * x, axis=-1, keepdims=True)
        xn = x * lax.rsqrt(ms + EPS) * g_ref[...]
        m = mod_ref[...]
        h_ref[...] = (xn * (1.0 + m[1:2]) + m[0:1]).astype(BF16)

    o_ref[...] = _dot(h_ref[...], w_ref[...])


def _proj(x, mod, g, w):
    b, t, d = x.shape
    tm = min(t, 1024)
    return pl.pallas_call(
        _proj_kernel,
        grid=(b, t // tm, PROJ_W // PROJ_TN),
        in_specs=[pl.BlockSpec((None, tm, d), lambda bi, i, j: (bi, i, 0)),
                  pl.BlockSpec((None, 6, d), lambda bi, i, j: (bi, 0, 0)),
                  pl.BlockSpec((1, d), lambda bi, i, j: (0, 0)),
                  pl.BlockSpec((d, PROJ_TN), lambda bi, i, j: (0, j))],
        out_specs=pl.BlockSpec((None, tm, PROJ_TN), lambda bi, i, j: (bi, i, j)),
        out_shape=jax.ShapeDtypeStruct((b, t, PROJ_W), F32),
        scratch_shapes=[pltpu.VMEM((tm, d), BF16)],
        compiler_params=_cparams(("arbitrary", "arbitrary", "arbitrary")),
        name="norm_proj",
    )(x, mod, g.reshape(1, d), w)


def _softplus(x):
    return jnp.maximum(x, 0.0) + jnp.log1p(jnp.exp(-jnp.abs(x)))


def _ssd_kernel(xbc_ref, z_ref, dt_ref, hist_ref, h0_ref, cw_ref, cb_ref, dtb_ref, alog_ref,
                dsk_ref, ng_ref, e3_ref, y_ref, hout_ref, hcat_ref, tail_ref, *, L):
    c = pl.program_id(1)
    nc = pl.num_programs(1)

    @pl.when(c == 0)
    def _():
        tail_ref[...] = hist_ref[...]
        for g in range(SSM_GROUPS):
            hcat_ref[g] = h0_ref[g].T

    x = xbc_ref[...]
    hp = tail_ref[...]
    w = cw_ref[...]
    row8 = lax.broadcasted_iota(jnp.int32, (SUBLANE, CONV_DIM), 0)
    acc = cb_ref[...] + x * w[CONV_W - 1:CONV_W]
    for k in range(1, CONV_W):
        xr = pltpu.roll(x, k, axis=0)
        top = jnp.where(row8 < k, pltpu.roll(hp, k, axis=0), xr[:SUBLANE])
        xr = jnp.concatenate([top, xr[SUBLANE:]], axis=0)
        acc = acc + xr * w[CONV_W - 1 - k:CONV_W - k]
    tail_ref[...] = x[L - SUBLANE:]
    xc = acc * _sigmoid(acc)

    dt = _softplus(dt_ref[...][:, :SSM_HEADS] + dtb_ref[...])
    a = dt * (-jnp.exp(alog_ref[...]))
    row = lax.broadcasted_iota(jnp.int32, (L, L), 0)
    col = lax.broadcasted_iota(jnp.int32, (L, L), 1)
    causal = row >= col
    tril = jnp.where(causal, 1.0, 0.0).astype(BF16)
    cs = _dot(tril, jnp.concatenate(_split3(a), axis=1))
    acum = cs[:, :SSM_HEADS] + cs[:, SSM_HEADS:2 * SSM_HEADS] + cs[:, 2 * SSM_HEADS:]
    sel_r = lax.broadcasted_iota(jnp.int32, (SSM_HEADS, 3 * SSM_HEADS), 0)
    sel_c = lax.broadcasted_iota(jnp.int32, (SSM_HEADS, 3 * SSM_HEADS), 1)
    sel = jnp.where(sel_c % SSM_HEADS == sel_r, 1.0, 0.0).astype(BF16)
    acum_t = _dot_nt(sel, jnp.concatenate(_split3(acum), axis=1))
    a_last = acum[L - 1:L]
    fac = jnp.concatenate([dt, jnp.exp(acum), jnp.exp(a_last - acum)], axis=0)
    fac3 = jnp.concatenate(_split3(fac), axis=1)
    lane_head = lax.broadcasted_iota(jnp.int32, (L, GROUP_W), 1) // SSM_HEADDIM

    for g in range(SSM_GROUPS):
        lo, hi = g * GROUP_W, (g + 1) * GROUP_W
        xs_g = xc[:, lo:hi]
        bm_g = xc[:, D_INNER + g * D_STATE:D_INNER + (g + 1) * D_STATE].astype(BF16)
        cm_g = xc[:, D_INNER + SSM_GROUPS * D_STATE + g * D_STATE:
                  D_INNER + SSM_GROUPS * D_STATE + (g + 1) * D_STATE].astype(BF16)
        fx = _dot(fac3, e3_ref[:, lo:hi])
        dt_x, ea_x, te_x = fx[:L], fx[L:2 * L], fx[2 * L:]
        xdt = xs_g * dt_x
        xdt_b = xdt.astype(BF16)
        cb = _dot_nt(cm_g, bm_g)
        ms = []
        for r in range(HEADS_PER_GROUP):
            h = g * HEADS_PER_GROUP + r
            seg = acum[:, h:h + 1] - acum_t[h:h + 1, :]
            dec = jnp.exp(jnp.where(causal, seg, -jnp.inf))
            ms.append((cb * dec).astype(BF16))
        yy = _dot(jnp.concatenate(ms, axis=0), xdt_b)
        y = yy[3 * L:]
        for r in range(HEADS_PER_GROUP - 2, -1, -1):
            y = jnp.where(lane_head == r, yy[r * L:(r + 1) * L], y)
        hc = hcat_ref[g]
        y = y + _dot(cm_g, hc.astype(BF16)) * ea_x
        hcat_ref[g] = ea_x[L - 1:L] * hc + _dot_tn(bm_g, (te_x * xdt).astype(BF16))
        y = y + dsk_ref[:, lo:hi] * xs_g
        zg = z_ref[:, lo:hi]
        y = y * (zg * _sigmoid(zg))
        y = y * lax.rsqrt(jnp.mean(y * y, axis=-1, keepdims=True) + EPS) * ng_ref[:, lo:hi]
        y_ref[:, lo:hi] = y.astype(y_ref.dtype)

    @pl.when(c == nc - 1)
    def _():
        for g in range(SSM_GROUPS):
            hout_ref[g] = hcat_ref[g].T


def _ssd(proj, hist8, h0, conv_w, conv_b, dtb, alog, dsk_x, norm_g, e3):
    b, t, _ = proj.shape
    L = min(t, 256)
    nzw = D_INNER // GROUP_W
    kern = functools.partial(_ssd_kernel, L=L)
    full = lambda shape: pl.BlockSpec(shape, lambda bi, ci: (0,) * len(shape))
    return pl.pallas_call(
        kern,
        grid=(b, t // L),
        in_specs=[pl.BlockSpec((None, L, CONV_DIM), lambda bi, ci: (bi, ci, COL_XBC // CONV_DIM)),
                  pl.BlockSpec((None, L, D_INNER), lambda bi, ci: (bi, ci, COL_Z // D_INNER)),
                  pl.BlockSpec((None, L, DT_PAD), lambda bi, ci: (bi, ci, COL_DT // DT_PAD)),
                  pl.BlockSpec((None, SUBLANE, CONV_DIM), lambda bi, ci: (bi, 0, 0)),
                  pl.BlockSpec((None, SSM_GROUPS, GROUP_W, D_STATE), lambda bi, ci: (bi, 0, 0, 0)),
                  full((CONV_W, CONV_DIM)), full((1, CONV_DIM)), full((1, SSM_HEADS)),
                  full((1, SSM_HEADS)), full((1, D_INNER)), full((1, D_INNER)),
                  full((3 * SSM_HEADS, D_INNER))],
        out_specs=[pl.BlockSpec((None, L, D_INNER), lambda bi, ci: (bi, ci, 0)),
                   pl.BlockSpec((None, SSM_GROUPS, GROUP_W, D_STATE), lambda bi, ci: (bi, 0, 0, 0))],
        out_shape=[jax.ShapeDtypeStruct((b, t, D_INNER), BF16),
                   jax.ShapeDtypeStruct((b, SSM_GROUPS, GROUP_W, D_STATE), F32)],
        scratch_shapes=[pltpu.VMEM((SSM_GROUPS, D_STATE, GROUP_W), F32),
                        pltpu.VMEM((SUBLANE, CONV_DIM), F32)],
        compiler_params=_cparams(("arbitrary", "arbitrary")),
        name="ssd",
    )(proj, proj, proj, hist8, h0, conv_w, conv_b, dtb, alog, dsk_x, norm_g, e3)


def _attn_kernel(sink_ref, q_ref, kp_ref, vp_ref, kc_ref, vc_ref, o_ref, *, tq, npast, banded):
    t = pl.program_id(1)
    nk = npast + tq
    q = q_ref[...]
    k = jnp.concatenate([kp_ref[...], kc_ref[...]], axis=0).astype(BF16)
    v = jnp.concatenate([vp_ref[...], vc_ref[...]], axis=0).astype(BF16)
    if banded:
        qi = lax.broadcasted_iota(jnp.int32, (tq, nk), 0)
        kj = lax.broadcasted_iota(jnp.int32, (tq, nk), 1)
        qchunk = (t * tq + qi) // CHUNK
        kpos = t * tq - npast + kj
        kchunk = (kpos + npast) // CHUNK - npast // CHUNK
        valid1 = (kpos >= 0) & (kchunk >= qchunk - WINDOW // CHUNK) & (kchunk <= qchunk)
        valid = jnp.concatenate([valid1] * Q_PER_KV, axis=0)
    for g in range(N_KV):
        kg = k[:, g * HEAD_DIM:(g + 1) * HEAD_DIM]
        vg = v[:, g * HEAD_DIM:(g + 1) * HEAD_DIM]
        heads = [g * Q_PER_KV + r for r in range(Q_PER_KV)]
        qs = jnp.concatenate([q[:, h * HEAD_DIM:(h + 1) * HEAD_DIM] for h in heads], axis=0).astype(BF16)
        s = _dot_nt(qs, kg) * ATTN_SCALE
        if banded:
            s = jnp.where(valid, s, -jnp.inf)
        sink = jnp.concatenate([jnp.full((tq, 1), sink_ref[h], F32) for h in heads], axis=0)
        m = jnp.maximum(jnp.max(s, axis=-1, keepdims=True), sink)
        e = jnp.exp(s - m)
        den = jnp.sum(e, axis=-1, keepdims=True) + jnp.exp(sink - m)
        p = (e * (1.0 / den)).astype(BF16)
        o = _dot(p, vg)
        for r, h in enumerate(heads):
            o_ref[:, h * HEAD_DIM:(h + 1) * HEAD_DIM] = o[r * tq:(r + 1) * tq].astype(o_ref.dtype)


def _attn(proj, k_hist, v_hist, sinks):
    b, t, _ = proj.shape
    banded = k_hist is None
    tq = min(t, 256)
    npast = WINDOW
    kw, kb, vb = KV_DIM, COL_K // KV_DIM, COL_V // KV_DIM
    if banded:
        blocks_per_tile = tq // npast
        prev = lambda col: pl.BlockSpec(
            (None, npast, kw), lambda bi, ti: (bi, jnp.maximum(ti * blocks_per_tile - 1, 0), col))
        kp_spec, vp_spec, kp, vp = prev(kb), prev(vb), proj, proj
    else:
        kp_spec = vp_spec = pl.BlockSpec((None, npast, kw), lambda bi, ti: (bi, 0, 0))
        kp, vp = k_hist, v_hist
    kern = functools.partial(_attn_kernel, tq=tq, npast=npast, banded=banded)
    return pl.pallas_call(
        kern,
        grid=(b, t // tq),
        in_specs=[pl.BlockSpec(memory_space=pltpu.SMEM),
                  pl.BlockSpec((None, tq, ATTN_DIM), lambda bi, ti: (bi, ti, COL_Q // ATTN_DIM)),
                  kp_spec, vp_spec,
                  pl.BlockSpec((None, tq, kw), lambda bi, ti: (bi, ti, kb)),
                  pl.BlockSpec((None, tq, kw), lambda bi, ti: (bi, ti, vb))],
        out_specs=pl.BlockSpec((None, tq, ATTN_DIM), lambda bi, ti: (bi, ti, 0)),
        out_shape=jax.ShapeDtypeStruct((b, t, ATTN_DIM), BF16),
        compiler_params=_cparams(("arbitrary", "arbitrary")),
        name="attn",
    )(sinks, proj, kp, vp, proj, proj)


def _merge_kernel(x_ref, y_ref, o_ref, gate_ref, mod_ref, g2_ref, wbs_ref, wba_ref, wo_ref,
                  x2_ref, h2_ref):
    d = x_ref.shape[-1]
    a = _dot(y_ref[...], wbs_ref[...])
    b = _dot(o_ref[...], wba_ref[...])
    gt = gate_ref[...]
    merged = _sigmoid(gt[:, :d]) * a + _sigmoid(gt[:, d:]) * b
    m = mod_ref[...]
    x2 = x_ref[...] + m[2:3] * _dot(merged.astype(BF16), wo_ref[...])
    x2_ref[...] = x2
    xn = x2 * lax.rsqrt(jnp.mean(x2 * x2, axis=-1, keepdims=True) + EPS) * g2_ref[...]
    h2_ref[...] = (xn * (1.0 + m[4:5]) + m[3:4]).astype(h2_ref.dtype)


def _merge(x, y_ssm, o_attn, proj, mod, g2, wbs, wba, wo):
    b, t, d = x.shape
    tm = min(t, 512)
    full = lambda shape: pl.BlockSpec(shape, lambda bi, i: (0,) * len(shape))
    return pl.pallas_call(
        _merge_kernel,
        grid=(b, t // tm),
        in_specs=[pl.BlockSpec((None, tm, d), lambda bi, i: (bi, i, 0)),
                  pl.BlockSpec((None, tm, D_INNER), lambda bi, i: (bi, i, 0)),
                  pl.BlockSpec((None, tm, ATTN_DIM), lambda bi, i: (bi, i, 0)),
                  pl.BlockSpec((None, tm, 2 * d), lambda bi, i: (bi, i, COL_GATE // (2 * d))),
                  pl.BlockSpec((None, 6, d), lambda bi, i: (bi, 0, 0)),
                  full((1, d)), full((D_INNER, d)), full((ATTN_DIM, d)), full((d, d))],
        out_specs=[pl.BlockSpec((None, tm, d), lambda bi, i: (bi, i, 0)),
                   pl.BlockSpec((None, tm, d), lambda bi, i: (bi, i, 0))],
        out_shape=[jax.ShapeDtypeStruct((b, t, d), F32), jax.ShapeDtypeStruct((b, t, d), BF16)],
        compiler_params=_cparams(("arbitrary", "arbitrary")),
        name="merge",
    )(x, y_ssm, o_attn, proj, mod, g2.reshape(1, d), wbs, wba, wo)


def _topk_values(s, k):
    rows = lax.broadcasted_iota(jnp.int32, s.shape, 0)
    out = []
    for _ in range(k):
        m = jnp.max(s, axis=0, keepdims=True)
        out.append(m)
        first = jnp.min(jnp.where(s == m, rows, s.shape[0]), axis=0, keepdims=True)
        s = jnp.where(rows == first, -jnp.inf, s)
    return out


def _gelu_tanh(x):
    return 0.5 * x * (1.0 + jnp.tanh(0.7978845608028654 * (x + 0.044715 * (x * x * x))))


def _peer_kernel(h2_ref, wq_ref, keys_ref, u_ref, vt_ref, o_ref,
                 s2_ref, thr_ref, ai_ref, bj_ref, acc_ref, s_ref, coef_ref, *, eb):
    e = pl.program_id(1)
    ne = pl.num_programs(1)
    isub = eb // N_KEYS

    @pl.when(e == 0)
    def _():
        q = _dot(h2_ref[...], wq_ref[...])
        for h in range(PEER_HEADS):
            q1 = q[:, (2 * h) * D_HALF:(2 * h + 1) * D_HALF].astype(BF16)
            q2 = q[:, (2 * h + 1) * D_HALF:(2 * h + 2) * D_HALF].astype(BF16)
            s1 = _dot_nt(keys_ref[2 * h], q1)
            s2 = _dot_nt(keys_ref[2 * h + 1], q2)
            v1 = _topk_values(s1, PEER_TOPK)
            v2 = _topk_values(s2, PEER_TOPK)
            v2m = jnp.concatenate(v2, axis=0)
            comb = jnp.concatenate([v1[i] + v2m for i in range(PEER_TOPK)], axis=0)
            cv = _topk_values(comb, PEER_TOPK + 1)
            z = jnp.exp(cv[0] - cv[0])
            for kk in range(1, PEER_TOPK):
                z = z + jnp.exp(cv[kk] - cv[0])
            tau = 0.5 * (cv[PEER_TOPK - 1] + cv[PEER_TOPK])
            s2_ref[h] = s2
            thr_ref[h] = tau - s1
            ai_ref[h] = jnp.exp(s1 - v1[0])
            bj_ref[h] = jnp.exp(s2 - v2[0]) * (1.0 / z)
        acc_ref[...] = jnp.zeros_like(acc_ref)

    s_ref[...] = _dot_nt(u_ref[...], h2_ref[...])

    def sub(ii, carry):
        i = e * isub + ii
        r0 = pl.multiple_of(ii * N_KEYS, N_KEYS)
        act = _gelu_tanh(s_ref[pl.ds(r0, N_KEYS), :])
        w = jnp.zeros_like(act)
        for h in range(PEER_HEADS):
            thr = thr_ref[h, pl.ds(i, 1), :]
            ai = ai_ref[h, pl.ds(i, 1), :]
            w = w + jnp.where(s2_ref[h] >= thr, ai * bj_ref[h], 0.0)
        coef_ref[pl.ds(r0, N_KEYS), :] = (w * act).astype(BF16)
        return carry

    lax.fori_loop(0, isub, sub, 0)
    acc_ref[...] += _dot(vt_ref[...], coef_ref[...])

    @pl.when(e == ne - 1)
    def _():
        o_ref[...] = acc_ref[...].T


def _peer(h2, wq, keys, u, vt):
    n, d = h2.shape
    tb = min(n, 512)
    eb = 2048
    kern = functools.partial(_peer_kernel, eb=eb)
    hs = (PEER_HEADS, N_KEYS, tb)
    return pl.pallas_call(
        kern,
        grid=(n // tb, N_EXPERTS // eb),
        in_specs=[pl.BlockSpec((tb, d), lambda i, e: (i, 0)),
                  pl.BlockSpec(wq.shape, lambda i, e: (0, 0)),
                  pl.BlockSpec(keys.shape, lambda i, e: (0, 0, 0)),
                  pl.BlockSpec((eb, d), lambda i, e: (e, 0)),
                  pl.BlockSpec((d, eb), lambda i, e: (0, e))],
        out_specs=pl.BlockSpec((tb, d), lambda i, e: (i, 0)),
        out_shape=jax.ShapeDtypeStruct((n, d), F32),
        scratch_shapes=[pltpu.VMEM(hs, F32), pltpu.VMEM(hs, F32), pltpu.VMEM(hs, F32),
                        pltpu.VMEM(hs, F32), pltpu.VMEM((d, tb), F32),
                        pltpu.VMEM((eb, tb), F32), pltpu.VMEM((eb, tb), BF16)],
        compiler_params=_cparams(("arbitrary", "arbitrary")),
        name="peer",
    )(h2, wq, keys, u, vt)


def _final_kernel(x2_ref, p_ref, mod_ref, g_ref, o_ref, *, normalize):
    x = x2_ref[...] + mod_ref[...][5:6] * p_ref[...]
    if normalize:
        x = x * lax.rsqrt(jnp.mean(x * x, axis=-1, keepdims=True) + EPS) * g_ref[...]
    o_ref[...] = x


def _final(x2, peer_out, mod, g, normalize):
    b, t, d = x2.shape
    tm = min(t, 1024)
    spec = pl.BlockSpec((None, tm, d), lambda bi, i: (bi, i, 0))
    return pl.pallas_call(
        functools.partial(_final_kernel, normalize=normalize),
        grid=(b, t // tm),
        in_specs=[spec, spec,
                  pl.BlockSpec((None, 6, d), lambda bi, i: (bi, 0, 0)),
                  pl.BlockSpec((1, d), lambda bi, i: (0, 0))],
        out_specs=spec,
        out_shape=jax.ShapeDtypeStruct((b, t, d), F32),
        compiler_params=_cparams(("arbitrary", "arbitrary")),
        name="final_norm",
    )(x2, peer_out, mod, g.reshape(1, d))


def _prep_layer_weights(w_in, conv_w, conv_b, dt_bias, a_log, d_skip, ssm_norm_g, w_branch_ssm,
                        w_branch_attn, w_out, peer_wq, peer_keys, peer_u, peer_v):
    d = w_in.shape[0]
    off_xbc = D_INNER
    off_dt = off_xbc + CONV_DIM
    off_q = off_dt + SSM_HEADS
    off_k = off_q + ATTN_DIM
    off_v = off_k + KV_DIM
    off_gate = off_v + KV_DIM
    w_re = jnp.concatenate([
        w_in[:, off_xbc:off_dt], w_in[:, :off_xbc], w_in[:, off_gate:], w_in[:, off_q:off_k],
        w_in[:, off_k:off_v], w_in[:, off_v:off_gate], w_in[:, off_dt:off_q],
        jnp.zeros((d, DT_PAD - SSM_HEADS), w_in.dtype)], axis=1).astype(BF16)
    head_of_lane = jnp.arange(D_INNER) // SSM_HEADDIM
    e1 = (jnp.arange(SSM_HEADS)[:, None] == head_of_lane[None, :]).astype(BF16)
    return dict(
        w_re=w_re, conv_w=conv_w, conv_b=conv_b.reshape(1, CONV_DIM),
        dtb=dt_bias.reshape(1, SSM_HEADS), alog=a_log.reshape(1, SSM_HEADS),
        dsk_x=jnp.repeat(d_skip, SSM_HEADDIM).reshape(1, D_INNER),
        norm_g=ssm_norm_g.reshape(1, D_INNER), e3=jnp.concatenate([e1, e1, e1], axis=0),
        wbs=w_branch_ssm.astype(BF16), wba=w_branch_attn.astype(BF16), wo=w_out.astype(BF16),
        wq=peer_wq.astype(BF16),
        keys=peer_keys.reshape(PEER_HEADS * 2, N_KEYS, D_HALF).astype(BF16),
        u=peer_u.astype(BF16), vt=peer_v.T.astype(BF16))


def _layer(x, mod, conv_hist, ssm0, k_hist, v_hist, norm1_g, norm2_g, sinks, lw):
    b, t, d = x.shape
    proj = _proj(x, mod, norm1_g, lw["w_re"])
    hist8 = jnp.pad(conv_hist, ((0, 0), (SUBLANE - (CONV_W - 1), 0), (0, 0)))
    h0 = ssm0.reshape(b, SSM_GROUPS, GROUP_W, D_STATE)
    y_ssm, h_new = _ssd(proj, hist8, h0, lw["conv_w"], lw["conv_b"], lw["dtb"], lw["alog"],
                        lw["dsk_x"], lw["norm_g"], lw["e3"])
    if k_hist is not None:
        k_hist = k_hist.reshape(b, -1, KV_DIM)
        v_hist = v_hist.reshape(b, -1, KV_DIM)
    o_attn = _attn(proj, k_hist, v_hist, sinks)
    x2, h2 = _merge(x, y_ssm, o_attn, proj, mod, norm2_g, lw["wbs"], lw["wba"], lw["wo"])
    peer_out = _peer(h2.reshape(b * t, d), lw["wq"], lw["keys"], lw["u"], lw["vt"]).reshape(b, t, d)
    keep = WINDOW if k_hist is None else t
    k_new = proj[:, t - keep:, COL_K:COL_K + KV_DIM].reshape(b, keep, N_KV, HEAD_DIM)
    v_new = proj[:, t - keep:, COL_V:COL_V + KV_DIM].reshape(b, keep, N_KV, HEAD_DIM)
    conv_new = proj[:, t - (CONV_W - 1):, COL_XBC:COL_XBC + CONV_DIM]
    ssm_new = h_new.reshape(b, SSM_HEADS, SSM_HEADDIM, D_STATE)
    return x2, peer_out, k_new, v_new, conv_new, ssm_new


def kernel(x_prompt, x_sample, cache_attn_k, cache_attn_v, state_conv, state_ssm, c_prompt, c_sample, ada_w, ada_b, norm1_g, w_in, conv_w, conv_b, dt_bias, a_log, d_skip, ssm_norm_g, attn_sinks, w_branch_ssm, w_branch_attn, w_out, norm2_g, peer_wq, peer_keys, peer_u, peer_v, final_g):
    depth = ada_w.shape[0]
    d = x_prompt.shape[-1]
    bp, bs = x_prompt.shape[0], x_sample.shape[0]
    xp, xs = x_prompt, x_sample
    c_all = jnp.concatenate([c_prompt, c_sample], axis=0)
    outs = [[] for _ in range(8)]
    for l in range(depth):
        lw = _prep_layer_weights(w_in[l], conv_w[l], conv_b[l], dt_bias[l], a_log[l], d_skip[l],
                                 ssm_norm_g[l], w_branch_ssm[l], w_branch_attn[l], w_out[l],
                                 peer_wq[l], peer_keys[l], peer_u[l], peer_v[l])
        mod = _adaln(c_all, ada_w[l], ada_b[l]).reshape(bp + bs, 6, d)
        last = l == depth - 1
        conv0 = jnp.zeros((bp, CONV_W - 1, CONV_DIM), F32)
        ssm0 = jnp.zeros((bp, SSM_HEADS, SSM_HEADDIM, D_STATE), F32)
        res = []
        for x, m, ch, s0, kh, vh in ((xp, mod[:bp], conv0, ssm0, None, None),
                                     (xs, mod[bp:], state_conv[l], state_ssm[l],
                                      cache_attn_k[l], cache_attn_v[l])):
            x2, peer_out, kn, vn, cn, sn = _layer(x, m, ch, s0, kh, vh, norm1_g[l], norm2_g[l],
                                                  attn_sinks[l], lw)
            res.append((_final(x2, peer_out, m, final_g, last), kn, vn, cn, sn))
        (xp, kp, vp, cp, sp), (xs, kn, vn, cn, sn) = res
        for lst, val in zip(outs, (kp, vp, cp, sp, kn, vn, cn, sn)):
            lst.append(val)
    return (xp, xs) + tuple(jnp.stack(o) for o in outs)
```

```python
import functools

import jax
import jax.numpy as jnp
from jax import lax
from jax.experimental import pallas as pl
from jax.experimental.pallas import tpu as pltpu

F32 = jnp.float32
BF16 = jnp.bfloat16

EPS = 1e-6
CHUNK = 64
WINDOW = 128
SSM_HEADS = 32
SSM_HEADDIM = 64
D_STATE = 128
SSM_GROUPS = 8
HEADS_PER_GROUP = SSM_HEADS // SSM_GROUPS
GROUP_W = HEADS_PER_GROUP * SSM_HEADDIM
D_INNER = SSM_HEADS * SSM_HEADDIM
CONV_W = 4
CONV_DIM = D_INNER + 2 * SSM_GROUPS * D_STATE
N_HEADS = 16
N_KV = 4
HEAD_DIM = 64
Q_PER_KV = N_HEADS // N_KV
ATTN_DIM = N_HEADS * HEAD_DIM
KV_DIM = N_KV * HEAD_DIM
ATTN_SCALE = HEAD_DIM ** -0.5
PEER_HEADS = 8
N_KEYS = 128
N_EXPERTS = N_KEYS * N_KEYS
PEER_TOPK = 16
D_HALF = 128

LANE = 128
SUBLANE = 8
DT_PAD = LANE

COL_XBC = 0
COL_Z = COL_XBC + CONV_DIM
COL_GATE = COL_Z + D_INNER
COL_Q = COL_GATE + 2048
COL_K = COL_Q + ATTN_DIM
COL_V = COL_K + KV_DIM
COL_DT = COL_V + KV_DIM
PROJ_W = COL_DT + DT_PAD
PROJ_TN = 1408

VMEM_LIMIT = 56 * 1024 * 1024


def _cparams(sem):
    return pltpu.CompilerParams(dimension_semantics=sem, vmem_limit_bytes=VMEM_LIMIT)


def _sigmoid(x):
    return 1.0 / (1.0 + jnp.exp(-x))


def _split2(v):
    hi = v.astype(BF16)
    lo = (v - hi.astype(F32)).astype(BF16)
    return hi, lo


def _split3(v):
    hi = v.astype(BF16)
    r = v - hi.astype(F32)
    mid = r.astype(BF16)
    lo = (r - mid.astype(F32)).astype(BF16)
    return hi, mid, lo


def _dot(a, b):
    return jnp.dot(a, b, preferred_element_type=F32)


def _dot_nt(a, b):
    return lax.dot_general(a, b, (((1,), (1,)), ((), ())), preferred_element_type=F32)


def _dot_tn(a, b):
    return lax.dot_general(a, b, (((0,), (0,)), ((), ())), preferred_element_type=F32)


def _adaln_kernel(c_ref, w_ref, b_ref, o_ref):
    c = c_ref[...]
    s = c * _sigmoid(c)
    s_hi, s_lo = _split2(s)
    w_hi, w_lo = _split2(w_ref[...])
    acc = _dot(s_hi, w_hi) + _dot(s_hi, w_lo) + _dot(s_lo, w_hi)
    o_ref[...] = acc + b_ref[...]


def _adaln(c, ada_w, ada_b):
    rows, d = c.shape
    n = ada_w.shape[1]
    tn = 1024
    return pl.pallas_call(
        _adaln_kernel,
        grid=(n // tn,),
        in_specs=[pl.BlockSpec((rows, d), lambda j: (0, 0)),
                  pl.BlockSpec((d, tn), lambda j: (0, j)),
                  pl.BlockSpec((1, tn), lambda j: (0, j))],
        out_specs=pl.BlockSpec((rows, tn), lambda j: (0, j)),
        out_shape=jax.ShapeDtypeStruct((rows, n), F32),
        compiler_params=_cparams(("arbitrary",)),
        name="adaln",
    )(c, ada_w, ada_b.reshape(1, n))


def _proj_kernel(x_ref, mod_ref, g_ref, w_ref, o_ref, h_ref):
    @pl.when(pl.program_id(2) == 0)
    def _():
        x = x_ref[...]
        ms = jnp.mean(x * x, axis=-1, keepdims=True)
        xn = x * lax.rsqrt(ms + EPS) * g_ref[...]
        m = mod_ref[...]
        h_ref[...] = (xn * (1.0 + m[1:2]) + m[0:1]).astype(BF16)

    o_ref[...] = _dot(h_ref[...], w_ref[...])


def _proj(x, mod, g, w):
    b, t, d = x.shape
    tm = min(t, 1024)
    return pl.pallas_call(
        _proj_kernel,
        grid=(b, t // tm, PROJ_W // PROJ_TN),
        in_specs=[pl.BlockSpec((None, tm, d), lambda bi, i, j: (bi, i, 0)),
                  pl.BlockSpec((None, 6, d), lambda bi, i, j: (bi, 0, 0)),
                  pl.BlockSpec((1, d), lambda bi, i, j: (0, 0)),
                  pl.BlockSpec((d, PROJ_TN), lambda bi, i, j: (0, j))],
        out_specs=pl.BlockSpec((None, tm, PROJ_TN), lambda bi, i, j: (bi, i, j)),
        out_shape=jax.ShapeDtypeStruct((b, t, PROJ_W), F32),
        scratch_shapes=[pltpu.VMEM((tm, d), BF16)],
        compiler_params=_cparams(("arbitrary", "arbitrary", "arbitrary")),
        name="norm_proj",
    )(x, mod, g.reshape(1, d), w)


def _softplus(x):
    return jnp.maximum(x, 0.0) + jnp.log1p(jnp.exp(-jnp.abs(x)))


def _ssd_kernel(xbc_ref, z_ref, dt_ref, hist_ref, h0_ref, cw_ref, cb_ref, dtb_ref, alog_ref,
                dsk_ref, ng_ref, e3_ref, y_ref, hout_ref, hcat_ref, tail_ref, *, L):
    c = pl.program_id(1)
    nc = pl.num_programs(1)

    @pl.when(c == 0)
    def _():
        tail_ref[...] = hist_ref[...]
        for g in range(SSM_GROUPS):
            hcat_ref[g] = h0_ref[g].T

    x = xbc_ref[...]
    hp = tail_ref[...]
    w = cw_ref[...]
    row8 = lax.broadcasted_iota(jnp.int32, (SUBLANE, CONV_DIM), 0)
    acc = cb_ref[...] + x * w[CONV_W - 1:CONV_W]
    for k in range(1, CONV_W):
        xr = pltpu.roll(x, k, axis=0)
        top = jnp.where(row8 < k, pltpu.roll(hp, k, axis=0), xr[:SUBLANE])
        xr = jnp.concatenate([top, xr[SUBLANE:]], axis=0)
        acc = acc + xr * w[CONV_W - 1 - k:CONV_W - k]
    tail_ref[...] = x[L - SUBLANE:]
    xc = acc * _sigmoid(acc)

    dt = _softplus(dt_ref[...][:, :SSM_HEADS] + dtb_ref[...])
    a = dt * (-jnp.exp(alog_ref[...]))
    row = lax.broadcasted_iota(jnp.int32, (L, L), 0)
    col = lax.broadcasted_iota(jnp.int32, (L, L), 1)
    causal = row >= col
    tril = jnp.where(causal, 1.0, 0.0).astype(BF16)
    cs = _dot(tril, jnp.concatenate(_split3(a), axis=1))
    acum = cs[:, :SSM_HEADS] + cs[:, SSM_HEADS:2 * SSM_HEADS] + cs[:, 2 * SSM_HEADS:]
    sel_r = lax.broadcasted_iota(jnp.int32, (SSM_HEADS, 3 * SSM_HEADS), 0)
    sel_c = lax.broadcasted_iota(jnp.int32, (SSM_HEADS, 3 * SSM_HEADS), 1)
    sel = jnp.where(sel_c % SSM_HEADS == sel_r, 1.0, 0.0).astype(BF16)
    acum_t = _dot_nt(sel, jnp.concatenate(_split3(acum), axis=1))
    a_last = acum[L - 1:L]
    fac = jnp.concatenate([dt, jnp.exp(acum), jnp.exp(a_last - acum)], axis=0)
    fac3 = jnp.concatenate(_split3(fac), axis=1)
    lane_head = lax.broadcasted_iota(jnp.int32, (L, GROUP_W), 1) // SSM_HEADDIM

    for g in range(SSM_GROUPS):
        lo, hi = g * GROUP_W, (g + 1) * GROUP_W
        xs_g = xc[:, lo:hi]
        bm_g = xc[:, D_INNER + g * D_STATE:D_INNER + (g + 1) * D_STATE].astype(BF16)
        cm_g = xc[:, D_INNER + SSM_GROUPS * D_STATE + g * D_STATE:
                  D_INNER + SSM_GROUPS * D_STATE + (g + 1) * D_STATE].astype(BF16)
        fx = _dot(fac3, e3_ref[:, lo:hi])
        dt_x, ea_x, te_x = fx[:L], fx[L:2 * L], fx[2 * L:]
        xdt = xs_g * dt_x
        xdt_b = xdt.astype(BF16)
        cb = _dot_nt(cm_g, bm_g)
        ms = []
        for r in range(HEADS_PER_GROUP):
            h = g * HEADS_PER_GROUP + r
            seg = acum[:, h:h + 1] - acum_t[h:h + 1, :]
            dec = jnp.exp(jnp.where(causal, seg, -jnp.inf))
            ms.append((cb * dec).astype(BF16))
        yy = _dot(jnp.concatenate(ms, axis=0), xdt_b)
        y = yy[3 * L:]
        for r in range(HEADS_PER_GROUP - 2, -1, -1):
            y = jnp.where(lane_head == r, yy[r * L:(r + 1) * L], y)
        hc = hcat_ref[g]
        y = y + _dot(cm_g, hc.astype(BF16)) * ea_x
        hcat_ref[g] = ea_x[L - 1:L] * hc + _dot_tn(bm_g, (te_x * xdt).astype(BF16))
        y = y + dsk_ref[:, lo:hi] * xs_g
        zg = z_ref[:, lo:hi]
        y = y * (zg * _sigmoid(zg))
        y = y * lax.rsqrt(jnp.mean(y * y, axis=-1, keepdims=True) + EPS) * ng_ref[:, lo:hi]
        y_ref[:, lo:hi] = y.astype(y_ref.dtype)

    @pl.when(c == nc - 1)
    def _():
        for g in range(SSM_GROUPS):
            hout_ref[g] = hcat_ref[g].T


def _ssd(proj, hist8, h0, conv_w, conv_b, dtb, alog, dsk_x, norm_g, e3):
    b, t, _ = proj.shape
    L = min(t, 256)
    nzw = D_INNER // GROUP_W
    kern = functools.partial(_ssd_kernel, L=L)
    full = lambda shape: pl.BlockSpec(shape, lambda bi, ci: (0,) * len(shape))
    return pl.pallas_call(
        kern,
        grid=(b, t // L),
        in_specs=[pl.BlockSpec((None, L, CONV_DIM), lambda bi, ci: (bi, ci, COL_XBC // CONV_DIM)),
                  pl.BlockSpec((None, L, D_INNER), lambda bi, ci: (bi, ci, COL_Z // D_INNER)),
                  pl.BlockSpec((None, L, DT_PAD), lambda bi, ci: (bi, ci, COL_DT // DT_PAD)),
                  pl.BlockSpec((None, SUBLANE, CONV_DIM), lambda bi, ci: (bi, 0, 0)),
                  pl.BlockSpec((None, SSM_GROUPS, GROUP_W, D_STATE), lambda bi, ci: (bi, 0, 0, 0)),
                  full((CONV_W, CONV_DIM)), full((1, CONV_DIM)), full((1, SSM_HEADS)),
                  full((1, SSM_HEADS)), full((1, D_INNER)), full((1, D_INNER)),
                  full((3 * SSM_HEADS, D_INNER))],
        out_specs=[pl.BlockSpec((None, L, D_INNER), lambda bi, ci: (bi, ci, 0)),
                   pl.BlockSpec((None, SSM_GROUPS, GROUP_W, D_STATE), lambda bi, ci: (bi, 0, 0, 0))],
        out_shape=[jax.ShapeDtypeStruct((b, t, D_INNER), BF16),
                   jax.ShapeDtypeStruct((b, SSM_GROUPS, GROUP_W, D_STATE), F32)],
        scratch_shapes=[pltpu.VMEM((SSM_GROUPS, D_STATE, GROUP_W), F32),
                        pltpu.VMEM((SUBLANE, CONV_DIM), F32)],
        compiler_params=_cparams(("arbitrary", "arbitrary")),
        name="ssd",
    )(proj, proj, proj, hist8, h0, conv_w, conv_b, dtb, alog, dsk_x, norm_g, e3)


def _attn_kernel(sink_ref, q_ref, kp_ref, vp_ref, kc_ref, vc_ref, o_ref, *, tq, npast, banded):
    t = pl.program_id(1)
    nk = npast + tq
    q = q_ref[...]
    k = jnp.concatenate([kp_ref[...], kc_ref[...]], axis=0).astype(BF16)
    v = jnp.concatenate([vp_ref[...], vc_ref[...]], axis=0).astype(BF16)
    if banded:
        qi = lax.broadcasted_iota(jnp.int32, (tq, nk), 0)
        kj = lax.broadcasted_iota(jnp.int32, (tq, nk), 1)
        qchunk = (t * tq + qi) // CHUNK
        kpos = t * tq - npast + kj
        kchunk = (kpos + npast) // CHUNK - npast // CHUNK
        valid1 = (kpos >= 0) & (kchunk >= qchunk - WINDOW // CHUNK) & (kchunk <= qchunk)
        valid = jnp.concatenate([valid1] * Q_PER_KV, axis=0)
    for g in range(N_KV):
        kg = k[:, g * HEAD_DIM:(g + 1) * HEAD_DIM]
        vg = v[:, g * HEAD_DIM:(g + 1) * HEAD_DIM]
        heads = [g * Q_PER_KV + r for r in range(Q_PER_KV)]
        qs = jnp.concatenate([q[:, h * HEAD_DIM:(h + 1) * HEAD_DIM] for h in heads], axis=0).astype(BF16)
        s = _dot_nt(qs, kg) * ATTN_SCALE
        if banded:
            s = jnp.where(valid, s, -jnp.inf)
        sink = jnp.concatenate([jnp.full((tq, 1), sink_ref[h], F32) for h in heads], axis=0)
        m = jnp.maximum(jnp.max(s, axis=-1, keepdims=True), sink)
        e = jnp.exp(s - m)
        den = jnp.sum(e, axis=-1, keepdims=True) + jnp.exp(sink - m)
        p = (e * (1.0 / den)).astype(BF16)
        o = _dot(p, vg)
        for r, h in enumerate(heads):
            o_ref[:, h * HEAD_DIM:(h + 1) * HEAD_DIM] = o[r * tq:(r + 1) * tq].astype(o_ref.dtype)


def _attn(proj, k_hist, v_hist, sinks):
    b, t, _ = proj.shape
    banded = k_hist is None
    tq = min(t, 256)
    npast = WINDOW
    kw, kb, vb = KV_DIM, COL_K // KV_DIM, COL_V // KV_DIM
    if banded:
        blocks_per_tile = tq // npast
        prev = lambda col: pl.BlockSpec(
            (None, npast, kw), lambda bi, ti: (bi, jnp.maximum(ti * blocks_per_tile - 1, 0), col))
        kp_spec, vp_spec, kp, vp = prev(kb), prev(vb), proj, proj
    else:
        kp_spec = vp_spec = pl.BlockSpec((None, npast, kw), lambda bi, ti: (bi, 0, 0))
        kp, vp = k_hist, v_hist
    kern = functools.partial(_attn_kernel, tq=tq, npast=npast, banded=banded)
    return pl.pallas_call(
        kern,
        grid=(b, t // tq),
        in_specs=[pl.BlockSpec(memory_space=pltpu.SMEM),
                  pl.BlockSpec((None, tq, ATTN_DIM), lambda bi, ti: (bi, ti, COL_Q // ATTN_DIM)),
                  kp_spec, vp_spec,
                  pl.BlockSpec((None, tq, kw), lambda bi, ti: (bi, ti, kb)),
                  pl.BlockSpec((None, tq, kw), lambda bi, ti: (bi, ti, vb))],
        out_specs=pl.BlockSpec((None, tq, ATTN_DIM), lambda bi, ti: (bi, ti, 0)),
        out_shape=jax.ShapeDtypeStruct((b, t, ATTN_DIM), BF16),
        compiler_params=_cparams(("arbitrary", "arbitrary")),
        name="attn",
    )(sinks, proj, kp, vp, proj, proj)


def _merge_kernel(x_ref, y_ref, o_ref, gate_ref, mod_ref, g2_ref, wbs_ref, wba_ref, wo_ref,
                  x2_ref, h2_ref):
    d = x_ref.shape[-1]
    a = _dot(y_ref[...], wbs_ref[...])
    b = _dot(o_ref[...], wba_ref[...])
    gt = gate_ref[...]
    merged = _sigmoid(gt[:, :d]) * a + _sigmoid(gt[:, d:]) * b
    m = mod_ref[...]
    x2 = x_ref[...] + m[2:3] * _dot(merged.astype(BF16), wo_ref[...])
    x2_ref[...] = x2
    xn = x2 * lax.rsqrt(jnp.mean(x2 * x2, axis=-1, keepdims=True) + EPS) * g2_ref[...]
    h2_ref[...] = (xn * (1.0 + m[4:5]) + m[3:4]).astype(h2_ref.dtype)


def _merge(x, y_ssm, o_attn, proj, mod, g2, wbs, wba, wo):
    b, t, d = x.shape
    tm = min(t, 512)
    full = lambda shape: pl.BlockSpec(shape, lambda bi, i: (0,) * len(shape))
    return pl.pallas_call(
        _merge_kernel,
        grid=(b, t // tm),
        in_specs=[pl.BlockSpec((None, tm, d), lambda bi, i: (bi, i, 0)),
                  pl.BlockSpec((None, tm, D_INNER), lambda bi, i: (bi, i, 0)),
                  pl.BlockSpec((None, tm, ATTN_DIM), lambda bi, i: (bi, i, 0)),
                  pl.BlockSpec((None, tm, 2 * d), lambda bi, i: (bi, i, COL_GATE // (2 * d))),
                  pl.BlockSpec((None, 6, d), lambda bi, i: (bi, 0, 0)),
                  full((1, d)), full((D_INNER, d)), full((ATTN_DIM, d)), full((d, d))],
        out_specs=[pl.BlockSpec((None, tm, d), lambda bi, i: (bi, i, 0)),
                   pl.BlockSpec((None, tm, d), lambda bi, i: (bi, i, 0))],
        out_shape=[jax.ShapeDtypeStruct((b, t, d), F32), jax.ShapeDtypeStruct((b, t, d), BF16)],
        compiler_params=_cparams(("arbitrary", "arbitrary")),
        name="merge",
    )(x, y_ssm, o_attn, proj, mod, g2.reshape(1, d), wbs, wba, wo)


def _batcher_pairs(n):
    pairs = []
    p = 1
    while p < n:
        k = p
        while k >= 1:
            for j in range(k % p, n - k, 2 * k):
                for i in range(min(k, n - j - k)):
                    if (i + j) // (2 * p) == (i + j + k) // (2 * p):
                        pairs.append((i + j, i + j + k))
            k //= 2
        p *= 2
    return tuple(pairs)


_SORT16 = _batcher_pairs(PEER_TOPK)


def _cmpx(rows, i, j):
    rows[i], rows[j] = jnp.maximum(rows[i], rows[j]), jnp.minimum(rows[i], rows[j])


def _top17(s):
    n = s.shape[1]
    rows = [s[r * SUBLANE:(r + 1) * SUBLANE] for r in range(PEER_TOPK)]
    for i, j in _SORT16:
        _cmpx(rows, i, j)
    x17 = jnp.full((SUBLANE, n), -jnp.inf, F32)
    for sh in (4, 2, 1):
        other = [pltpu.roll(rows[PEER_TOPK - 1 - r], sh, axis=0) for r in range(PEER_TOPK)]
        lo = [jnp.minimum(rows[r], other[r]) for r in range(PEER_TOPK)]
        rows = [jnp.maximum(rows[r], other[r]) for r in range(PEER_TOPK)]
        lomax = functools.reduce(jnp.maximum, lo)
        x17 = jnp.maximum(jnp.maximum(x17, pltpu.roll(x17, sh, axis=0)), lomax)
        for dd in (8, 4, 2, 1):
            for i in range(PEER_TOPK):
                if i & dd == 0:
                    _cmpx(rows, i, i + dd)
    return rows, x17


def _select_threshold(v1, x1, v2, x2):
    n = v1[0].shape[1]
    sub = lax.broadcasted_iota(jnp.int32, (SUBLANE, n), 0)

    def pack(vals):
        out = vals[SUBLANE - 1]
        for b in range(SUBLANE - 2, -1, -1):
            out = jnp.where(sub == b, vals[b], out)
        return out

    v2lo, v2hi, v1hi = pack(v2[:SUBLANE]), pack(v2[SUBLANE:]), pack(v1[SUBLANE:])
    cands = ([v1[0] + v2lo, v1[0] + v2hi] + [v1[a] + v2lo for a in range(1, SUBLANE)]
             + [v1hi + v2[0]])
    m0 = (v1[0] + v2[0])[0:1]
    zero = jnp.zeros((1, n), F32)
    cum, z = zero, zero
    c16 = jnp.full((1, n), -jnp.inf, F32)
    c17 = c16
    for _ in range(PEER_TOPK + 1):
        m = jnp.max(functools.reduce(jnp.maximum, cands), axis=0, keepdims=True)
        eqs = [c == m for c in cands]
        cnt = jnp.sum(functools.reduce(jnp.add, [jnp.where(q, 1.0, 0.0) for q in eqs]),
                      axis=0, keepdims=True)
        cands = [jnp.where(q, -jnp.inf, c) for q, c in zip(eqs, cands)]
        prev, cum = cum, cum + cnt
        z = z + jnp.minimum(cnt, jnp.maximum(PEER_TOPK - prev, 0.0)) * jnp.exp(m - m0)
        c16 = jnp.where((prev < PEER_TOPK) & (cum >= PEER_TOPK), m, c16)
        c17 = jnp.where((prev < PEER_TOPK + 1) & (cum >= PEER_TOPK + 1), m, c17)
    c17 = jnp.maximum(c17, jnp.maximum(x1 + v2[0], v1[0] + x2)[0:1])
    return 0.5 * (c16 + c17), z


def _gelu_tanh(x):
    return 0.5 * x * (1.0 + jnp.tanh(0.7978845608028654 * (x + 0.044715 * (x * x * x))))


PEER_EB = 2048
PEER_SB = 512


def _peer_kernel(h2_ref, wqt_ref, keys_ref, u_ref, vt_ref, o_ref,
                 h2t_ref, s2_ref, thr_ref, ai_ref, bj_ref, acc_ref):
    e = pl.program_id(1)
    ne = pl.num_programs(1)

    @pl.when(e == 0)
    def _():
        h2t_ref[...] = h2_ref[...].astype(F32).T.astype(BF16)
        qt = _dot(wqt_ref[...], h2t_ref[...])
        for h in range(PEER_HEADS):
            q1 = qt[(2 * h) * D_HALF:(2 * h + 1) * D_HALF].astype(BF16)
            q2 = qt[(2 * h + 1) * D_HALF:(2 * h + 2) * D_HALF].astype(BF16)
            s1 = _dot(keys_ref[2 * h], q1)
            s2 = _dot(keys_ref[2 * h + 1], q2)
            v1, x1 = _top17(s1)
            v2, x2 = _top17(s2)
            tau, z = _select_threshold(v1, x1, v2, x2)
            s2_ref[h] = s2
            thr_ref[h] = tau - s1
            ai_ref[h] = jnp.exp(s1 - v1[0][0:1])
            bj_ref[h] = jnp.exp(s2 - v2[0][0:1]) * (1.0 / z)
        acc_ref[...] = jnp.zeros_like(acc_ref)

    h2t = h2t_ref[...]
    nsb = PEER_EB // PEER_SB

    def pre_acts(k):
        return _dot(u_ref[k * PEER_SB:(k + 1) * PEER_SB, :], h2t)

    def coefficients(k, s):
        coef = []
        for ii in range(PEER_SB // N_KEYS):
            i = e * (PEER_EB // N_KEYS) + k * (PEER_SB // N_KEYS) + ii
            act = _gelu_tanh(s[ii * N_KEYS:(ii + 1) * N_KEYS])
            w = None
            for h in range(PEER_HEADS):
                hit = s2_ref[h] >= thr_ref[h, pl.ds(i, 1), :]
                wh = jnp.where(hit, ai_ref[h, pl.ds(i, 1), :] * bj_ref[h], 0.0)
                w = wh if w is None else w + wh
            coef.append((w * act).astype(BF16))
        return jnp.concatenate(coef, axis=0)

    def accumulate(k, coef):
        acc_ref[...] += _dot(vt_ref[:, k * PEER_SB:(k + 1) * PEER_SB], coef)

    s_next, coef_prev = pre_acts(0), None
    for k in range(nsb):
        s_cur = s_next
        if k + 1 < nsb:
            s_next = pre_acts(k + 1)
        if coef_prev is not None:
            accumulate(k - 1, coef_prev)
        coef_prev = coefficients(k, s_cur)
    accumulate(nsb - 1, coef_prev)

    @pl.when(e == ne - 1)
    def _():
        o_ref[...] = acc_ref[...].T


def _peer(h2, wqt, keys, u, vt):
    n, d = h2.shape
    tb = min(n, 512)
    hs = (PEER_HEADS, N_KEYS, tb)
    return pl.pallas_call(
        _peer_kernel,
        grid=(n // tb, N_EXPERTS // PEER_EB),
        in_specs=[pl.BlockSpec((tb, d), lambda i, e: (i, 0)),
                  pl.BlockSpec(wqt.shape, lambda i, e: (0, 0)),
                  pl.BlockSpec(keys.shape, lambda i, e: (0, 0, 0)),
                  pl.BlockSpec((PEER_EB, d), lambda i, e: (e, 0)),
                  pl.BlockSpec((d, PEER_EB), lambda i, e: (0, e))],
        out_specs=pl.BlockSpec((tb, d), lambda i, e: (i, 0)),
        out_shape=jax.ShapeDtypeStruct((n, d), F32),
        scratch_shapes=[pltpu.VMEM((d, tb), BF16), pltpu.VMEM(hs, F32), pltpu.VMEM(hs, F32),
                        pltpu.VMEM(hs, F32), pltpu.VMEM(hs, F32), pltpu.VMEM((d, tb), F32)],
        compiler_params=_cparams(("arbitrary", "arbitrary")),
        name="peer",
    )(h2, wqt, keys, u, vt)


def _final_kernel(x2_ref, p_ref, mod_ref, g_ref, o_ref, *, normalize):
    x = x2_ref[...] + mod_ref[...][5:6] * p_ref[...]
    if normalize:
        x = x * lax.rsqrt(jnp.mean(x * x, axis=-1, keepdims=True) + EPS) * g_ref[...]
    o_ref[...] = x


def _final(x2, peer_out, mod, g, normalize):
    b, t, d = x2.shape
    tm = min(t, 1024)
    spec = pl.BlockSpec((None, tm, d), lambda bi, i: (bi, i, 0))
    return pl.pallas_call(
        functools.partial(_final_kernel, normalize=normalize),
        grid=(b, t // tm),
        in_specs=[spec, spec,
                  pl.BlockSpec((None, 6, d), lambda bi, i: (bi, 0, 0)),
                  pl.BlockSpec((1, d), lambda bi, i: (0, 0))],
        out_specs=spec,
        out_shape=jax.ShapeDtypeStruct((b, t, d), F32),
        compiler_params=_cparams(("arbitrary", "arbitrary")),
        name="final_norm",
    )(x2, peer_out, mod, g.reshape(1, d))


def _prep_layer_weights(w_in, conv_w, conv_b, dt_bias, a_log, d_skip, ssm_norm_g, w_branch_ssm,
                        w_branch_attn, w_out, peer_wq, peer_keys, peer_u, peer_v):
    d = w_in.shape[0]
    off_xbc = D_INNER
    off_dt = off_xbc + CONV_DIM
    off_q = off_dt + SSM_HEADS
    off_k = off_q + ATTN_DIM
    off_v = off_k + KV_DIM
    off_gate = off_v + KV_DIM
    w_re = jnp.concatenate([
        w_in[:, off_xbc:off_dt], w_in[:, :off_xbc], w_in[:, off_gate:], w_in[:, off_q:off_k],
        w_in[:, off_k:off_v], w_in[:, off_v:off_gate], w_in[:, off_dt:off_q],
        jnp.zeros((d, DT_PAD - SSM_HEADS), w_in.dtype)], axis=1).astype(BF16)
    head_of_lane = jnp.arange(D_INNER) // SSM_HEADDIM
    e1 = (jnp.arange(SSM_HEADS)[:, None] == head_of_lane[None, :]).astype(BF16)
    return dict(
        w_re=w_re, conv_w=conv_w, conv_b=conv_b.reshape(1, CONV_DIM),
        dtb=dt_bias.reshape(1, SSM_HEADS), alog=a_log.reshape(1, SSM_HEADS),
        dsk_x=jnp.repeat(d_skip, SSM_HEADDIM).reshape(1, D_INNER),
        norm_g=ssm_norm_g.reshape(1, D_INNER), e3=jnp.concatenate([e1, e1, e1], axis=0),
        wbs=w_branch_ssm.astype(BF16), wba=w_branch_attn.astype(BF16), wo=w_out.astype(BF16),
        wqt=peer_wq.T.astype(BF16),
        keys=peer_keys.reshape(PEER_HEADS * 2, N_KEYS, D_HALF).astype(BF16),
        u=peer_u.astype(BF16), vt=peer_v.T.astype(BF16))


def _layer(x, mod, conv_hist, ssm0, k_hist, v_hist, norm1_g, norm2_g, sinks, lw):
    b, t, d = x.shape
    proj = _proj(x, mod, norm1_g, lw["w_re"])
    hist8 = jnp.pad(conv_hist, ((0, 0), (SUBLANE - (CONV_W - 1), 0), (0, 0)))
    h0 = ssm0.reshape(b, SSM_GROUPS, GROUP_W, D_STATE)
    y_ssm, h_new = _ssd(proj, hist8, h0, lw["conv_w"], lw["conv_b"], lw["dtb"], lw["alog"],
                        lw["dsk_x"], lw["norm_g"], lw["e3"])
    if k_hist is not None:
        k_hist = k_hist.reshape(b, -1, KV_DIM)
        v_hist = v_hist.reshape(b, -1, KV_DIM)
    o_attn = _attn(proj, k_hist, v_hist, sinks)
    x2, h2 = _merge(x, y_ssm, o_attn, proj, mod, norm2_g, lw["wbs"], lw["wba"], lw["wo"])
    peer_out = _peer(h2.reshape(b * t, d), lw["wqt"], lw["keys"], lw["u"], lw["vt"]).reshape(b, t, d)
    keep = WINDOW if k_hist is None else t
    k_new = proj[:, t - keep:, COL_K:COL_K + KV_DIM].reshape(b, keep, N_KV, HEAD_DIM)
    v_new = proj[:, t - keep:, COL_V:COL_V + KV_DIM].reshape(b, keep, N_KV, HEAD_DIM)
    conv_new = proj[:, t - (CONV_W - 1):, COL_XBC:COL_XBC + CONV_DIM]
    ssm_new = h_new.reshape(b, SSM_HEADS, SSM_HEADDIM, D_STATE)
    return x2, peer_out, k_new, v_new, conv_new, ssm_new


def kernel(x_prompt, x_sample, cache_attn_k, cache_attn_v, state_conv, state_ssm, c_prompt, c_sample, ada_w, ada_b, norm1_g, w_in, conv_w, conv_b, dt_bias, a_log, d_skip, ssm_norm_g, attn_sinks, w_branch_ssm, w_branch_attn, w_out, norm2_g, peer_wq, peer_keys, peer_u, peer_v, final_g):
    depth = ada_w.shape[0]
    d = x_prompt.shape[-1]
    bp, bs = x_prompt.shape[0], x_sample.shape[0]
    xp, xs = x_prompt, x_sample
    c_all = jnp.concatenate([c_prompt, c_sample], axis=0)
    outs = [[] for _ in range(8)]
    for l in range(depth):
        lw = _prep_layer_weights(w_in[l], conv_w[l], conv_b[l], dt_bias[l], a_log[l], d_skip[l],
                                 ssm_norm_g[l], w_branch_ssm[l], w_branch_attn[l], w_out[l],
                                 peer_wq[l], peer_keys[l], peer_u[l], peer_v[l])
        mod = _adaln(c_all, ada_w[l], ada_b[l]).reshape(bp + bs, 6, d)
        last = l == depth - 1
        conv0 = jnp.zeros((bp, CONV_W - 1, CONV_DIM), F32)
        ssm0 = jnp.zeros((bp, SSM_HEADS, SSM_HEADDIM, D_STATE), F32)
        res = []
        for x, m, ch, s0, kh, vh in ((xp, mod[:bp], conv0, ssm0, None, None),
                                     (xs, mod[bp:], state_conv[l], state_ssm[l],
                                      cache_attn_k[l], cache_attn_v[l])):
            x2, peer_out, kn, vn, cn, sn = _layer(x, m, ch, s0, kh, vh, norm1_g[l], norm2_g[l],
                                                  attn_sinks[l], lw)
            res.append((_final(x2, peer_out, m, final_g, last), kn, vn, cn, sn))
        (xp, kp, vp, cp, sp), (xs, kn, vn, cn, sn) = res
        for lst, val in zip(outs, (kp, vp, cp, sp, kn, vn, cn, sn)):
            lst.append(val)
    return (xp, xs) + tuple(jnp.stack(o) for o in outs)
```

```python
import functools

import jax
import jax.numpy as jnp
from jax import lax
from jax.experimental import pallas as pl
from jax.experimental.pallas import tpu as pltpu

F32 = jnp.float32
BF16 = jnp.bfloat16

EPS = 1e-6
CHUNK = 64
WINDOW = 128
SSM_HEADS = 32
SSM_HEADDIM = 64
D_STATE = 128
SSM_GROUPS = 8
HEADS_PER_GROUP = SSM_HEADS // SSM_GROUPS
GROUP_W = HEADS_PER_GROUP * SSM_HEADDIM
D_INNER = SSM_HEADS * SSM_HEADDIM
CONV_W = 4
CONV_DIM = D_INNER + 2 * SSM_GROUPS * D_STATE
N_HEADS = 16
N_KV = 4
HEAD_DIM = 64
Q_PER_KV = N_HEADS // N_KV
ATTN_DIM = N_HEADS * HEAD_DIM
KV_DIM = N_KV * HEAD_DIM
ATTN_SCALE = HEAD_DIM ** -0.5
PEER_HEADS = 8
N_KEYS = 128
N_EXPERTS = N_KEYS * N_KEYS
PEER_TOPK = 16
D_HALF = 128

LANE = 128
SUBLANE = 8
DT_PAD = LANE

COL_XBC = 0
COL_Z = COL_XBC + CONV_DIM
COL_GATE = COL_Z + D_INNER
COL_Q = COL_GATE + 2048
COL_K = COL_Q + ATTN_DIM
COL_V = COL_K + KV_DIM
COL_DT = COL_V + KV_DIM
PROJ_W = COL_DT + DT_PAD
PROJ_TN = 1408

VMEM_LIMIT = 56 * 1024 * 1024


def _cparams(sem):
    return pltpu.CompilerParams(dimension_semantics=sem, vmem_limit_bytes=VMEM_LIMIT)


def _sigmoid(x):
    return 1.0 / (1.0 + jnp.exp(-x))


def _split2(v):
    hi = v.astype(BF16)
    lo = (v - hi.astype(F32)).astype(BF16)
    return hi, lo


def _split3(v):
    hi = v.astype(BF16)
    r = v - hi.astype(F32)
    mid = r.astype(BF16)
    lo = (r - mid.astype(F32)).astype(BF16)
    return hi, mid, lo


def _dot(a, b):
    return jnp.dot(a, b, preferred_element_type=F32)


def _dot_nt(a, b):
    return lax.dot_general(a, b, (((1,), (1,)), ((), ())), preferred_element_type=F32)


def _dot_tn(a, b):
    return lax.dot_general(a, b, (((0,), (0,)), ((), ())), preferred_element_type=F32)


def _adaln_kernel(c_ref, w_ref, b_ref, o_ref):
    c = c_ref[...]
    s = c * _sigmoid(c)
    s_hi, s_lo = _split2(s)
    w_hi, w_lo = _split2(w_ref[...])
    acc = _dot(s_hi, w_hi) + _dot(s_hi, w_lo) + _dot(s_lo, w_hi)
    o_ref[...] = acc + b_ref[...]


def _adaln(c, ada_w, ada_b):
    rows, d = c.shape
    n = ada_w.shape[1]
    tn = 1024
    return pl.pallas_call(
        _adaln_kernel,
        grid=(n // tn,),
        in_specs=[pl.BlockSpec((rows, d), lambda j: (0, 0)),
                  pl.BlockSpec((d, tn), lambda j: (0, j)),
                  pl.BlockSpec((1, tn), lambda j: (0, j))],
        out_specs=pl.BlockSpec((rows, tn), lambda j: (0, j)),
        out_shape=jax.ShapeDtypeStruct((rows, n), F32),
        compiler_params=_cparams(("arbitrary",)),
        name="adaln",
    )(c, ada_w, ada_b.reshape(1, n))


def _proj_kernel(x_ref, mod_ref, g_ref, w_ref, o_ref, h_ref):
    @pl.when(pl.program_id(2) == 0)
    def _():
        x = x_ref[...]
        ms = jnp.mean(x * x, axis=-1, keepdims=True)
        xn = x * lax.rsqrt(ms + EPS) * g_ref[...]
        m = mod_ref[...]
        h_ref[...] = (xn * (1.0 + m[1:2]) + m[0:1]).astype(BF16)

    o_ref[...] = _dot(h_ref[...], w_ref[...])


def _proj(x, mod, g, w):
    b, t, d = x.shape
    tm = min(t, 1024)
    return pl.pallas_call(
        _proj_kernel,
        grid=(b, t // tm, PROJ_W // PROJ_TN),
        in_specs=[pl.BlockSpec((None, tm, d), lambda bi, i, j: (bi, i, 0)),
                  pl.BlockSpec((None, 6, d), lambda bi, i, j: (bi, 0, 0)),
                  pl.BlockSpec((1, d), lambda bi, i, j: (0, 0)),
                  pl.BlockSpec((d, PROJ_TN), lambda bi, i, j: (0, j))],
        out_specs=pl.BlockSpec((None, tm, PROJ_TN), lambda bi, i, j: (bi, i, j)),
        out_shape=jax.ShapeDtypeStruct((b, t, PROJ_W), F32),
        scratch_shapes=[pltpu.VMEM((tm, d), BF16)],
        compiler_params=_cparams(("arbitrary", "arbitrary", "arbitrary")),
        name="norm_proj",
    )(x, mod, g.reshape(1, d), w)


def _softplus(x):
    return jnp.maximum(x, 0.0) + jnp.log1p(jnp.exp(-jnp.abs(x)))


def _ssd_kernel(xbc_ref, z_ref, dt_ref, hist_ref, h0_ref, cw_ref, cb_ref, dtb_ref, alog_ref,
                dsk_ref, ng_ref, e3_ref, y_ref, hout_ref, hcat_ref, tail_ref, *, L):
    c = pl.program_id(1)
    nc = pl.num_programs(1)

    @pl.when(c == 0)
    def _():
        tail_ref[...] = hist_ref[...]
        for g in range(SSM_GROUPS):
            hcat_ref[g] = h0_ref[g].T

    x = xbc_ref[...]
    hp = tail_ref[...]
    w = cw_ref[...]
    row8 = lax.broadcasted_iota(jnp.int32, (SUBLANE, CONV_DIM), 0)
    acc = cb_ref[...] + x * w[CONV_W - 1:CONV_W]
    for k in range(1, CONV_W):
        xr = pltpu.roll(x, k, axis=0)
        top = jnp.where(row8 < k, pltpu.roll(hp, k, axis=0), xr[:SUBLANE])
        xr = jnp.concatenate([top, xr[SUBLANE:]], axis=0)
        acc = acc + xr * w[CONV_W - 1 - k:CONV_W - k]
    tail_ref[...] = x[L - SUBLANE:]
    xc = acc * _sigmoid(acc)

    dt = _softplus(dt_ref[...][:, :SSM_HEADS] + dtb_ref[...])
    a = dt * (-jnp.exp(alog_ref[...]))
    row = lax.broadcasted_iota(jnp.int32, (L, L), 0)
    col = lax.broadcasted_iota(jnp.int32, (L, L), 1)
    causal = row >= col
    tril = jnp.where(causal, 1.0, 0.0).astype(BF16)
    cs = _dot(tril, jnp.concatenate(_split3(a), axis=1))
    acum = cs[:, :SSM_HEADS] + cs[:, SSM_HEADS:2 * SSM_HEADS] + cs[:, 2 * SSM_HEADS:]
    sel_r = lax.broadcasted_iota(jnp.int32, (SSM_HEADS, 3 * SSM_HEADS), 0)
    sel_c = lax.broadcasted_iota(jnp.int32, (SSM_HEADS, 3 * SSM_HEADS), 1)
    sel = jnp.where(sel_c % SSM_HEADS == sel_r, 1.0, 0.0).astype(BF16)
    acum_t = _dot_nt(sel, jnp.concatenate(_split3(acum), axis=1))
    a_last = acum[L - 1:L]
    fac = jnp.concatenate([dt, jnp.exp(acum), jnp.exp(a_last - acum)], axis=0)
    fac3 = jnp.concatenate(_split3(fac), axis=1)
    lane_head = lax.broadcasted_iota(jnp.int32, (L, GROUP_W), 1) // SSM_HEADDIM

    for g in range(SSM_GROUPS):
        lo, hi = g * GROUP_W, (g + 1) * GROUP_W
        xs_g = xc[:, lo:hi]
        bm_g = xc[:, D_INNER + g * D_STATE:D_INNER + (g + 1) * D_STATE].astype(BF16)
        cm_g = xc[:, D_INNER + SSM_GROUPS * D_STATE + g * D_STATE:
                  D_INNER + SSM_GROUPS * D_STATE + (g + 1) * D_STATE].astype(BF16)
        fx = _dot(fac3, e3_ref[:, lo:hi])
        dt_x, ea_x, te_x = fx[:L], fx[L:2 * L], fx[2 * L:]
        xdt = xs_g * dt_x
        xdt_b = xdt.astype(BF16)
        cb = _dot_nt(cm_g, bm_g)
        ms = []
        for r in range(HEADS_PER_GROUP):
            h = g * HEADS_PER_GROUP + r
            seg = acum[:, h:h + 1] - acum_t[h:h + 1, :]
            dec = jnp.exp(jnp.where(causal, seg, -jnp.inf))
            ms.append((cb * dec).astype(BF16))
        yy = _dot(jnp.concatenate(ms, axis=0), xdt_b)
        y = yy[3 * L:]
        for r in range(HEADS_PER_GROUP - 2, -1, -1):
            y = jnp.where(lane_head == r, yy[r * L:(r + 1) * L], y)
        hc = hcat_ref[g]
        y = y + _dot(cm_g, hc.astype(BF16)) * ea_x
        hcat_ref[g] = ea_x[L - 1:L] * hc + _dot_tn(bm_g, (te_x * xdt).astype(BF16))
        y = y + dsk_ref[:, lo:hi] * xs_g
        zg = z_ref[:, lo:hi]
        y = y * (zg * _sigmoid(zg))
        y = y * lax.rsqrt(jnp.mean(y * y, axis=-1, keepdims=True) + EPS) * ng_ref[:, lo:hi]
        y_ref[:, lo:hi] = y.astype(y_ref.dtype)

    @pl.when(c == nc - 1)
    def _():
        for g in range(SSM_GROUPS):
            hout_ref[g] = hcat_ref[g].T


def _ssd(proj, hist8, h0, conv_w, conv_b, dtb, alog, dsk_x, norm_g, e3):
    b, t, _ = proj.shape
    L = min(t, 256)
    nzw = D_INNER // GROUP_W
    kern = functools.partial(_ssd_kernel, L=L)
    full = lambda shape: pl.BlockSpec(shape, lambda bi, ci: (0,) * len(shape))
    return pl.pallas_call(
        kern,
        grid=(b, t // L),
        in_specs=[pl.BlockSpec((None, L, CONV_DIM), lambda bi, ci: (bi, ci, COL_XBC // CONV_DIM)),
                  pl.BlockSpec((None, L, D_INNER), lambda bi, ci: (bi, ci, COL_Z // D_INNER)),
                  pl.BlockSpec((None, L, DT_PAD), lambda bi, ci: (bi, ci, COL_DT // DT_PAD)),
                  pl.BlockSpec((None, SUBLANE, CONV_DIM), lambda bi, ci: (bi, 0, 0)),
                  pl.BlockSpec((None, SSM_GROUPS, GROUP_W, D_STATE), lambda bi, ci: (bi, 0, 0, 0)),
                  full((CONV_W, CONV_DIM)), full((1, CONV_DIM)), full((1, SSM_HEADS)),
                  full((1, SSM_HEADS)), full((1, D_INNER)), full((1, D_INNER)),
                  full((3 * SSM_HEADS, D_INNER))],
        out_specs=[pl.BlockSpec((None, L, D_INNER), lambda bi, ci: (bi, ci, 0)),
                   pl.BlockSpec((None, SSM_GROUPS, GROUP_W, D_STATE), lambda bi, ci: (bi, 0, 0, 0))],
        out_shape=[jax.ShapeDtypeStruct((b, t, D_INNER), BF16),
                   jax.ShapeDtypeStruct((b, SSM_GROUPS, GROUP_W, D_STATE), F32)],
        scratch_shapes=[pltpu.VMEM((SSM_GROUPS, D_STATE, GROUP_W), F32),
                        pltpu.VMEM((SUBLANE, CONV_DIM), F32)],
        compiler_params=_cparams(("arbitrary", "arbitrary")),
        name="ssd",
    )(proj, proj, proj, hist8, h0, conv_w, conv_b, dtb, alog, dsk_x, norm_g, e3)


def _attn_kernel(sink_ref, q_ref, kp_ref, vp_ref, kc_ref, vc_ref, o_ref, *, tq, npast, banded):
    t = pl.program_id(1)
    cq = CHUNK if banded else tq
    nkw = npast + cq
    q = (q_ref[...] * ATTN_SCALE).astype(BF16)
    k = jnp.concatenate([kp_ref[...], kc_ref[...]], axis=0).astype(BF16)
    v = jnp.concatenate([vp_ref[...], vc_ref[...]], axis=0).astype(BF16)
    blocks = [(c, g) for c in range(tq // cq) for g in range(N_KV)]
    kj = lax.broadcasted_iota(jnp.int32, (1, nkw), 1)

    scores = []
    for c, g in blocks:
        qs = jnp.concatenate([q[c * cq:(c + 1) * cq, (g * Q_PER_KV + r) * HEAD_DIM:
                                (g * Q_PER_KV + r + 1) * HEAD_DIM] for r in range(Q_PER_KV)], axis=0)
        s = _dot_nt(qs, k[c * cq:c * cq + nkw, g * HEAD_DIM:(g + 1) * HEAD_DIM])
        if banded:
            s = s + jnp.where(t * tq + c * cq - npast + kj >= 0, 0.0, -jnp.inf)
        scores.append(s)
    probs = []
    for (c, g), s in zip(blocks, scores):
        sink = jnp.concatenate([jnp.full((cq, 1), sink_ref[g * Q_PER_KV + r], F32)
                                for r in range(Q_PER_KV)], axis=0)
        m = jnp.maximum(jnp.max(s, axis=-1, keepdims=True), sink)
        e = jnp.exp(s - m)
        den = jnp.sum(e, axis=-1, keepdims=True) + jnp.exp(sink - m)
        probs.append((e.astype(BF16), 1.0 / den))
    for (c, g), (e, inv) in zip(blocks, probs):
        o = _dot(e, v[c * cq:c * cq + nkw, g * HEAD_DIM:(g + 1) * HEAD_DIM]) * inv
        for r in range(Q_PER_KV):
            h = g * Q_PER_KV + r
            o_ref[c * cq:(c + 1) * cq, h * HEAD_DIM:(h + 1) * HEAD_DIM] = (
                o[r * cq:(r + 1) * cq].astype(o_ref.dtype))


def _attn(proj, k_hist, v_hist, sinks):
    b, t, _ = proj.shape
    banded = k_hist is None
    tq = min(t, 256)
    npast = WINDOW
    kw, kb, vb = KV_DIM, COL_K // KV_DIM, COL_V // KV_DIM
    if banded:
        blocks_per_tile = tq // npast
        prev = lambda col: pl.BlockSpec(
            (None, npast, kw), lambda bi, ti: (bi, jnp.maximum(ti * blocks_per_tile - 1, 0), col))
        kp_spec, vp_spec, kp, vp = prev(kb), prev(vb), proj, proj
    else:
        kp_spec = vp_spec = pl.BlockSpec((None, npast, kw), lambda bi, ti: (bi, 0, 0))
        kp, vp = k_hist, v_hist
    kern = functools.partial(_attn_kernel, tq=tq, npast=npast, banded=banded)
    return pl.pallas_call(
        kern,
        grid=(b, t // tq),
        in_specs=[pl.BlockSpec(memory_space=pltpu.SMEM),
                  pl.BlockSpec((None, tq, ATTN_DIM), lambda bi, ti: (bi, ti, COL_Q // ATTN_DIM)),
                  kp_spec, vp_spec,
                  pl.BlockSpec((None, tq, kw), lambda bi, ti: (bi, ti, kb)),
                  pl.BlockSpec((None, tq, kw), lambda bi, ti: (bi, ti, vb))],
        out_specs=pl.BlockSpec((None, tq, ATTN_DIM), lambda bi, ti: (bi, ti, 0)),
        out_shape=jax.ShapeDtypeStruct((b, t, ATTN_DIM), BF16),
        compiler_params=_cparams(("arbitrary", "arbitrary")),
        name="attn",
    )(sinks, proj, kp, vp, proj, proj)


def _merge_kernel(x_ref, y_ref, o_ref, gate_ref, mod_ref, g2_ref, wbs_ref, wba_ref, wo_ref,
                  x2_ref, h2_ref):
    d = x_ref.shape[-1]
    a = _dot(y_ref[...], wbs_ref[...])
    b = _dot(o_ref[...], wba_ref[...])
    gt = gate_ref[...]
    merged = _sigmoid(gt[:, :d]) * a + _sigmoid(gt[:, d:]) * b
    m = mod_ref[...]
    x2 = x_ref[...] + m[2:3] * _dot(merged.astype(BF16), wo_ref[...])
    x2_ref[...] = x2
    xn = x2 * lax.rsqrt(jnp.mean(x2 * x2, axis=-1, keepdims=True) + EPS) * g2_ref[...]
    h2_ref[...] = (xn * (1.0 + m[4:5]) + m[3:4]).astype(h2_ref.dtype)


def _merge(x, y_ssm, o_attn, proj, mod, g2, wbs, wba, wo):
    b, t, d = x.shape
    tm = min(t, 512)
    full = lambda shape: pl.BlockSpec(shape, lambda bi, i: (0,) * len(shape))
    return pl.pallas_call(
        _merge_kernel,
        grid=(b, t // tm),
        in_specs=[pl.BlockSpec((None, tm, d), lambda bi, i: (bi, i, 0)),
                  pl.BlockSpec((None, tm, D_INNER), lambda bi, i: (bi, i, 0)),
                  pl.BlockSpec((None, tm, ATTN_DIM), lambda bi, i: (bi, i, 0)),
                  pl.BlockSpec((None, tm, 2 * d), lambda bi, i: (bi, i, COL_GATE // (2 * d))),
                  pl.BlockSpec((None, 6, d), lambda bi, i: (bi, 0, 0)),
                  full((1, d)), full((D_INNER, d)), full((ATTN_DIM, d)), full((d, d))],
        out_specs=[pl.BlockSpec((None, tm, d), lambda bi, i: (bi, i, 0)),
                   pl.BlockSpec((None, tm, d), lambda bi, i: (bi, i, 0))],
        out_shape=[jax.ShapeDtypeStruct((b, t, d), F32), jax.ShapeDtypeStruct((b, t, d), BF16)],
        compiler_params=_cparams(("arbitrary", "arbitrary")),
        name="merge",
    )(x, y_ssm, o_attn, proj, mod, g2.reshape(1, d), wbs, wba, wo)


def _batcher_pairs(n):
    pairs = []
    p = 1
    while p < n:
        k = p
        while k >= 1:
            for j in range(k % p, n - k, 2 * k):
                for i in range(min(k, n - j - k)):
                    if (i + j) // (2 * p) == (i + j + k) // (2 * p):
                        pairs.append((i + j, i + j + k))
            k //= 2
        p *= 2
    return tuple(pairs)


_SORT16 = _batcher_pairs(PEER_TOPK)


def _max2(a, b):
    return b if a is None else a if b is None else jnp.maximum(a, b)


def _min2(a, b):
    return None if a is None or b is None else jnp.minimum(a, b)


def _cmpx(rows, i, j):
    rows[i], rows[j] = _max2(rows[i], rows[j]), _min2(rows[i], rows[j])


def _top17(rows):
    rows = list(rows)
    for i, j in _SORT16:
        _cmpx(rows, i, j)
    x17 = None
    for sh in (4, 2, 1):
        roll = lambda a: None if a is None else pltpu.roll(a, sh, axis=0)
        other = [roll(rows[PEER_TOPK - 1 - r]) for r in range(PEER_TOPK)]
        lo = [_min2(rows[r], other[r]) for r in range(PEER_TOPK)]
        rows = [_max2(rows[r], other[r]) for r in range(PEER_TOPK)]
        x17 = functools.reduce(_max2, lo + [x17, roll(x17)])
        for dd in (8, 4, 2, 1):
            for i in range(PEER_TOPK):
                if i & dd == 0:
                    _cmpx(rows, i, i + dd)
    return rows, x17


def _select_threshold(v1, x1, v2, x2):
    n = v1[0].shape[1]
    sub = lax.broadcasted_iota(jnp.int32, (SUBLANE, n), 0)

    def pack(vals):
        out = vals[SUBLANE - 1]
        for b in range(SUBLANE - 2, -1, -1):
            out = jnp.where(sub == b, vals[b], out)
        return out

    v2lo, v2hi, v1hi = pack(v2[:SUBLANE]), pack(v2[SUBLANE:]), pack(v1[SUBLANE:])
    cands = ([v1[0] + v2lo, v1[0] + v2hi] + [v1[a] + v2lo for a in range(1, SUBLANE)]
             + [v1hi + v2[0]])
    m0 = (v1[0] + v2[0])[0:1]
    zero = jnp.zeros((1, n), F32)
    cum, z = zero, zero
    c16 = jnp.full((1, n), -jnp.inf, F32)
    c17 = c16
    for _ in range(PEER_TOPK + 1):
        m = jnp.max(functools.reduce(jnp.maximum, cands), axis=0, keepdims=True)
        eqs = [c == m for c in cands]
        cnt = jnp.sum(functools.reduce(jnp.add, [jnp.where(q, 1.0, 0.0) for q in eqs]),
                      axis=0, keepdims=True)
        cands = [jnp.where(q, -jnp.inf, c) for q, c in zip(eqs, cands)]
        prev, cum = cum, cum + cnt
        z = z + jnp.minimum(cnt, jnp.maximum(PEER_TOPK - prev, 0.0)) * jnp.exp(m - m0)
        c16 = jnp.where((prev < PEER_TOPK) & (cum >= PEER_TOPK), m, c16)
        c17 = jnp.where((prev < PEER_TOPK + 1) & (cum >= PEER_TOPK + 1), m, c17)
    c17 = jnp.maximum(c17, jnp.maximum(x1 + v2[0], v1[0] + x2)[0:1])
    return 0.5 * (c16 + c17), z


GELU_C0 = 0.7978845608028654
GELU_C1 = GELU_C0 * 0.044715


def _gelu_tanh_x2(x):
    return x * (1.0 + jnp.tanh(x * (GELU_C0 + GELU_C1 * (x * x))))


PEER_EB = 2048
PEER_SB = 512


def _peer_kernel(h2_ref, wqt_ref, keys_ref, u_ref, vt_ref, o_ref,
                 h2t_ref, s2_ref, thr_ref, aiw_ref, bj_ref, acc_ref):
    e = pl.program_id(1)
    ne = pl.num_programs(1)
    tb = h2_ref.shape[0]

    @pl.when(e == 0)
    def _():
        h2t_ref[...] = h2_ref[...].astype(F32).T.astype(BF16)
        qt = _dot(wqt_ref[...], h2t_ref[...])
        for h in range(PEER_HEADS):
            q1 = qt[(2 * h) * D_HALF:(2 * h + 1) * D_HALF].astype(BF16)
            q2 = qt[(2 * h + 1) * D_HALF:(2 * h + 2) * D_HALF].astype(BF16)
            s1 = _dot(keys_ref[2 * h], q1)
            s2 = _dot(keys_ref[2 * h + 1], q2)
            v1, x1 = _top17([s1[r * SUBLANE:(r + 1) * SUBLANE] for r in range(PEER_TOPK)])
            v2, x2 = _top17([s2[r * SUBLANE:(r + 1) * SUBLANE] for r in range(PEER_TOPK)])
            tau, z = _select_threshold(v1, x1, v2, x2)
            s2_ref[h] = s2
            thr_ref[h] = tau - s1
            ai = jnp.exp(s1 - v1[0][0:1]).astype(BF16).astype(F32)
            bits = pltpu.bitcast(ai, jnp.uint32)
            aiw_ref[h] = bits | (bits >> 16)
            bj_ref[h] = (jnp.exp(s2 - v2[0][0:1]) * (0.5 / z)).astype(BF16)
        acc_ref[...] = jnp.zeros_like(acc_ref)

    h2t = h2t_ref[...]
    nsb = PEER_EB // PEER_SB

    def pre_acts(k):
        return _dot(u_ref[k * PEER_SB:(k + 1) * PEER_SB, :], h2t)

    def coefficients(k, s):
        coef = []
        for ii in range(PEER_SB // N_KEYS):
            i = e * (PEER_EB // N_KEYS) + k * (PEER_SB // N_KEYS) + ii
            act = _gelu_tanh_x2(s[ii * N_KEYS:(ii + 1) * N_KEYS]).astype(BF16)
            w = None
            for h in range(PEER_HEADS):
                hit = s2_ref[h] >= thr_ref[h, pl.ds(i, 1), :]
                ai = pltpu.bitcast(jnp.broadcast_to(aiw_ref[h, pl.ds(i, 1), :], (SUBLANE, tb)), BF16)
                ai = jnp.concatenate([ai] * (N_KEYS // (2 * SUBLANE)), axis=0)
                wh = jnp.where(hit, ai * bj_ref[h], jnp.zeros((), BF16))
                w = wh if w is None else w + wh
            coef.append(w * act)
        return jnp.concatenate(coef, axis=0)

    def accumulate(k, coef):
        acc_ref[...] += _dot(vt_ref[:, k * PEER_SB:(k + 1) * PEER_SB], coef)

    s_next, coef_prev = pre_acts(0), None
    for k in range(nsb):
        s_cur = s_next
        if k + 1 < nsb:
            s_next = pre_acts(k + 1)
        if coef_prev is not None:
            accumulate(k - 1, coef_prev)
        coef_prev = coefficients(k, s_cur)
    accumulate(nsb - 1, coef_prev)

    @pl.when(e == ne - 1)
    def _():
        o_ref[...] = acc_ref[...].T


def _peer(h2, wqt, keys, u, vt):
    n, d = h2.shape
    tb = min(n, 512)
    hs = (PEER_HEADS, N_KEYS, tb)
    return pl.pallas_call(
        _peer_kernel,
        grid=(n // tb, N_EXPERTS // PEER_EB),
        in_specs=[pl.BlockSpec((tb, d), lambda i, e: (i, 0)),
                  pl.BlockSpec(wqt.shape, lambda i, e: (0, 0)),
                  pl.BlockSpec(keys.shape, lambda i, e: (0, 0, 0)),
                  pl.BlockSpec((PEER_EB, d), lambda i, e: (e, 0)),
                  pl.BlockSpec((d, PEER_EB), lambda i, e: (0, e))],
        out_specs=pl.BlockSpec((tb, d), lambda i, e: (i, 0)),
        out_shape=jax.ShapeDtypeStruct((n, d), F32),
        scratch_shapes=[pltpu.VMEM((d, tb), BF16), pltpu.VMEM(hs, F32), pltpu.VMEM(hs, F32),
                        pltpu.VMEM(hs, jnp.uint32), pltpu.VMEM(hs, BF16), pltpu.VMEM((d, tb), F32)],
        compiler_params=_cparams(("arbitrary", "arbitrary")),
        name="peer",
    )(h2, wqt, keys, u, vt)


def _final_kernel(x2_ref, p_ref, mod_ref, g_ref, o_ref, *, normalize):
    x = x2_ref[...] + mod_ref[...][5:6] * p_ref[...]
    if normalize:
        x = x * lax.rsqrt(jnp.mean(x * x, axis=-1, keepdims=True) + EPS) * g_ref[...]
    o_ref[...] = x


def _final(x2, peer_out, mod, g, normalize):
    b, t, d = x2.shape
    tm = min(t, 1024)
    spec = pl.BlockSpec((None, tm, d), lambda bi, i: (bi, i, 0))
    return pl.pallas_call(
        functools.partial(_final_kernel, normalize=normalize),
        grid=(b, t // tm),
        in_specs=[spec, spec,
                  pl.BlockSpec((None, 6, d), lambda bi, i: (bi, 0, 0)),
                  pl.BlockSpec((1, d), lambda bi, i: (0, 0))],
        out_specs=spec,
        out_shape=jax.ShapeDtypeStruct((b, t, d), F32),
        compiler_params=_cparams(("arbitrary", "arbitrary")),
        name="final_norm",
    )(x2, peer_out, mod, g.reshape(1, d))


def _prep_layer_weights(w_in, conv_w, conv_b, dt_bias, a_log, d_skip, ssm_norm_g, w_branch_ssm,
                        w_branch_attn, w_out, peer_wq, peer_keys, peer_u, peer_v):
    d = w_in.shape[0]
    off_xbc = D_INNER
    off_dt = off_xbc + CONV_DIM
    off_q = off_dt + SSM_HEADS
    off_k = off_q + ATTN_DIM
    off_v = off_k + KV_DIM
    off_gate = off_v + KV_DIM
    w_re = jnp.concatenate([
        w_in[:, off_xbc:off_dt], w_in[:, :off_xbc], w_in[:, off_gate:], w_in[:, off_q:off_k],
        w_in[:, off_k:off_v], w_in[:, off_v:off_gate], w_in[:, off_dt:off_q],
        jnp.zeros((d, DT_PAD - SSM_HEADS), w_in.dtype)], axis=1).astype(BF16)
    head_of_lane = jnp.arange(D_INNER) // SSM_HEADDIM
    e1 = (jnp.arange(SSM_HEADS)[:, None] == head_of_lane[None, :]).astype(BF16)
    return dict(
        w_re=w_re, conv_w=conv_w, conv_b=conv_b.reshape(1, CONV_DIM),
        dtb=dt_bias.reshape(1, SSM_HEADS), alog=a_log.reshape(1, SSM_HEADS),
        dsk_x=jnp.repeat(d_skip, SSM_HEADDIM).reshape(1, D_INNER),
        norm_g=ssm_norm_g.reshape(1, D_INNER), e3=jnp.concatenate([e1, e1, e1], axis=0),
        wbs=w_branch_ssm.astype(BF16), wba=w_branch_attn.astype(BF16), wo=w_out.astype(BF16),
        wqt=peer_wq.T.astype(BF16),
        keys=peer_keys.reshape(PEER_HEADS * 2, N_KEYS, D_HALF).astype(BF16),
        u=peer_u.astype(BF16), vt=peer_v.T.astype(BF16))


def _layer(x, mod, conv_hist, ssm0, k_hist, v_hist, norm1_g, norm2_g, sinks, lw):
    b, t, d = x.shape
    proj = _proj(x, mod, norm1_g, lw["w_re"])
    hist8 = jnp.pad(conv_hist, ((0, 0), (SUBLANE - (CONV_W - 1), 0), (0, 0)))
    h0 = ssm0.reshape(b, SSM_GROUPS, GROUP_W, D_STATE)
    y_ssm, h_new = _ssd(proj, hist8, h0, lw["conv_w"], lw["conv_b"], lw["dtb"], lw["alog"],
                        lw["dsk_x"], lw["norm_g"], lw["e3"])
    if k_hist is not None:
        k_hist = k_hist.reshape(b, -1, KV_DIM)
        v_hist = v_hist.reshape(b, -1, KV_DIM)
    o_attn = _attn(proj, k_hist, v_hist, sinks)
    x2, h2 = _merge(x, y_ssm, o_attn, proj, mod, norm2_g, lw["wbs"], lw["wba"], lw["wo"])
    peer_out = _peer(h2.reshape(b * t, d), lw["wqt"], lw["keys"], lw["u"], lw["vt"]).reshape(b, t, d)
    keep = WINDOW if k_hist is None else t
    k_new = proj[:, t - keep:, COL_K:COL_K + KV_DIM].reshape(b, keep, N_KV, HEAD_DIM)
    v_new = proj[:, t - keep:, COL_V:COL_V + KV_DIM].reshape(b, keep, N_KV, HEAD_DIM)
    conv_new = proj[:, t - (CONV_W - 1):, COL_XBC:COL_XBC + CONV_DIM]
    ssm_new = h_new.reshape(b, SSM_HEADS, SSM_HEADDIM, D_STATE)
    return x2, peer_out, k_new, v_new, conv_new, ssm_new


def kernel(x_prompt, x_sample, cache_attn_k, cache_attn_v, state_conv, state_ssm, c_prompt, c_sample, ada_w, ada_b, norm1_g, w_in, conv_w, conv_b, dt_bias, a_log, d_skip, ssm_norm_g, attn_sinks, w_branch_ssm, w_branch_attn, w_out, norm2_g, peer_wq, peer_keys, peer_u, peer_v, final_g):
    depth = ada_w.shape[0]
    d = x_prompt.shape[-1]
    bp, bs = x_prompt.shape[0], x_sample.shape[0]
    xp, xs = x_prompt, x_sample
    c_all = jnp.concatenate([c_prompt, c_sample], axis=0)
    outs = [[] for _ in range(8)]
    for l in range(depth):
        lw = _prep_layer_weights(w_in[l], conv_w[l], conv_b[l], dt_bias[l], a_log[l], d_skip[l],
                                 ssm_norm_g[l], w_branch_ssm[l], w_branch_attn[l], w_out[l],
                                 peer_wq[l], peer_keys[l], peer_u[l], peer_v[l])
        mod = _adaln(c_all, ada_w[l], ada_b[l]).reshape(bp + bs, 6, d)
        last = l == depth - 1
        conv0 = jnp.zeros((bp, CONV_W - 1, CONV_DIM), F32)
        ssm0 = jnp.zeros((bp, SSM_HEADS, SSM_HEADDIM, D_STATE), F32)
        res = []
        for x, m, ch, s0, kh, vh in ((xp, mod[:bp], conv0, ssm0, None, None),
                                     (xs, mod[bp:], state_conv[l], state_ssm[l],
                                      cache_attn_k[l], cache_attn_v[l])):
            x2, peer_out, kn, vn, cn, sn = _layer(x, m, ch, s0, kh, vh, norm1_g[l], norm2_g[l],
                                                  attn_sinks[l], lw)
            res.append((_final(x2, peer_out, m, final_g, last), kn, vn, cn, sn))
        (xp, kp, vp, cp, sp), (xs, kn, vn, cn, sn) = res
        for lst, val in zip(outs, (kp, vp, cp, sp, kn, vn, cn, sn)):
            lst.append(val)
    return (xp, xs) + tuple(jnp.stack(o) for o in outs)
```

```python
import functools

import jax
import jax.numpy as jnp
from jax import lax
from jax.experimental import pallas as pl
from jax.experimental.pallas import tpu as pltpu

F32 = jnp.float32
BF16 = jnp.bfloat16

EPS = 1e-6
CHUNK = 64
WINDOW = 128
SSM_HEADS = 32
SSM_HEADDIM = 64
D_STATE = 128
SSM_GROUPS = 8
HEADS_PER_GROUP = SSM_HEADS // SSM_GROUPS
GROUP_W = HEADS_PER_GROUP * SSM_HEADDIM
D_INNER = SSM_HEADS * SSM_HEADDIM
CONV_W = 4
CONV_DIM = D_INNER + 2 * SSM_GROUPS * D_STATE
N_HEADS = 16
N_KV = 4
HEAD_DIM = 64
Q_PER_KV = N_HEADS // N_KV
ATTN_DIM = N_HEADS * HEAD_DIM
KV_DIM = N_KV * HEAD_DIM
ATTN_SCALE = HEAD_DIM ** -0.5
PEER_HEADS = 8
N_KEYS = 128
N_EXPERTS = N_KEYS * N_KEYS
PEER_TOPK = 16
D_HALF = 128

LANE = 128
SUBLANE = 8
DT_PAD = LANE

COL_XBC = 0
COL_Z = COL_XBC + CONV_DIM
COL_GATE = COL_Z + D_INNER
COL_Q = COL_GATE + 2048
COL_K = COL_Q + ATTN_DIM
COL_V = COL_K + KV_DIM
PROJ_W = COL_V + KV_DIM
PROJ_TN = 2432
CONV_CB = 1024

VMEM_LIMIT = 56 * 1024 * 1024


def _cparams(sem):
    return pltpu.CompilerParams(dimension_semantics=sem, vmem_limit_bytes=VMEM_LIMIT)


def _sigmoid(x):
    return 1.0 / (1.0 + jnp.exp(-x))


def _split2(v):
    hi = v.astype(BF16)
    lo = (v - hi.astype(F32)).astype(BF16)
    return hi, lo


def _split3(v):
    hi = v.astype(BF16)
    r = v - hi.astype(F32)
    mid = r.astype(BF16)
    lo = (r - mid.astype(F32)).astype(BF16)
    return hi, mid, lo


def _dot(a, b):
    return jnp.dot(a, b, preferred_element_type=F32)


def _dot_nt(a, b):
    return lax.dot_general(a, b, (((1,), (1,)), ((), ())), preferred_element_type=F32)


def _dot_tn(a, b):
    return lax.dot_general(a, b, (((0,), (0,)), ((), ())), preferred_element_type=F32)


def _adaln_kernel(c_ref, w_ref, b_ref, o_ref):
    c = c_ref[...]
    s = c * _sigmoid(c)
    s_hi, s_lo = _split2(s)
    w_hi, w_lo = _split2(w_ref[...])
    acc = _dot(s_hi, w_hi) + _dot(s_hi, w_lo) + _dot(s_lo, w_hi)
    o_ref[...] = acc + b_ref[...]


def _adaln(c, ada_w, ada_b):
    rows, d = c.shape
    n = ada_w.shape[1]
    tn = 1024
    return pl.pallas_call(
        _adaln_kernel,
        grid=(n // tn,),
        in_specs=[pl.BlockSpec((rows, d), lambda j: (0, 0)),
                  pl.BlockSpec((d, tn), lambda j: (0, j)),
                  pl.BlockSpec((1, tn), lambda j: (0, j))],
        out_specs=pl.BlockSpec((rows, tn), lambda j: (0, j)),
        out_shape=jax.ShapeDtypeStruct((rows, n), F32),
        compiler_params=_cparams(("arbitrary",)),
        name="adaln",
    )(c, ada_w, ada_b.reshape(1, n))


def _proj_kernel(x_ref, mod_ref, g_ref, w_ref, wdt_ref, o_ref, dt_ref, h_ref):
    @pl.when(pl.program_id(2) == 0)
    def _():
        x = x_ref[...]
        ms = jnp.mean(x * x, axis=-1, keepdims=True)
        xn = x * lax.rsqrt(ms + EPS) * g_ref[...]
        m = mod_ref[...]
        h_ref[...] = (xn * (1.0 + m[1:2]) + m[0:1]).astype(BF16)
        dt_ref[...] = _dot(h_ref[...], wdt_ref[...])

    o_ref[...] = _dot(h_ref[...], w_ref[...]).astype(o_ref.dtype)


def _proj(x, mod, g, w, w_dt):
    b, t, d = x.shape
    tm = min(t, 1024)
    return pl.pallas_call(
        _proj_kernel,
        grid=(b, t // tm, PROJ_W // PROJ_TN),
        in_specs=[pl.BlockSpec((None, tm, d), lambda bi, i, j: (bi, i, 0)),
                  pl.BlockSpec((None, 6, d), lambda bi, i, j: (bi, 0, 0)),
                  pl.BlockSpec((1, d), lambda bi, i, j: (0, 0)),
                  pl.BlockSpec((d, PROJ_TN), lambda bi, i, j: (0, j)),
                  pl.BlockSpec((d, DT_PAD), lambda bi, i, j: (0, 0))],
        out_specs=[pl.BlockSpec((None, tm, PROJ_TN), lambda bi, i, j: (bi, i, j)),
                   pl.BlockSpec((None, tm, DT_PAD), lambda bi, i, j: (bi, i, 0))],
        out_shape=[jax.ShapeDtypeStruct((b, t, PROJ_W), BF16),
                   jax.ShapeDtypeStruct((b, t, DT_PAD), F32)],
        scratch_shapes=[pltpu.VMEM((tm, d), BF16)],
        compiler_params=_cparams(("arbitrary", "arbitrary", "arbitrary")),
        name="norm_proj",
    )(x, mod, g.reshape(1, d), w, w_dt)


def _softplus(x):
    return jnp.maximum(x, 0.0) + jnp.log1p(jnp.exp(-jnp.abs(x)))


def _ssd_kernel(xbc_ref, z_ref, dt_ref, hist_ref, h0_ref, cw_ref, cb_ref, dtb_ref, alog_ref,
                dsk_ref, ng_ref, e3_ref, y_ref, hout_ref, hcat_ref, tail_ref, xc_ref, *, L):
    c = pl.program_id(1)
    nc = pl.num_programs(1)

    @pl.when(c == 0)
    def _():
        tail_ref[...] = hist_ref[...]
        for g in range(SSM_GROUPS):
            hcat_ref[g] = h0_ref[g].T

    row = lax.broadcasted_iota(jnp.int32, (L, L), 0)
    col = lax.broadcasted_iota(jnp.int32, (L, L), 1)
    causal = row >= col

    shifts = jnp.concatenate([jnp.where(row - col == k, 1.0, 0.0) for k in range(1, CONV_W)],
                             axis=0).astype(BF16)
    hp = tail_ref[...]
    row8 = lax.broadcasted_iota(jnp.int32, (SUBLANE, CONV_CB), 0)
    for cl in range(0, CONV_DIM, CONV_CB):
        ch = cl + CONV_CB
        xb = xbc_ref[:, cl:ch]
        x = xb.astype(F32)
        sh = _dot(shifts, xb)
        acc = cb_ref[:, cl:ch] + x * cw_ref[CONV_W - 1:CONV_W, cl:ch]
        for k in range(1, CONV_W):
            xr = sh[(k - 1) * L:k * L]
            top = jnp.where(row8 < k, pltpu.roll(hp[:, cl:ch], k, axis=0), xr[:SUBLANE])
            xr = jnp.concatenate([top, xr[SUBLANE:]], axis=0)
            acc = acc + xr * cw_ref[CONV_W - 1 - k:CONV_W - k, cl:ch]
        tail_ref[:, cl:ch] = x[L - SUBLANE:]
        xc_ref[:, cl:ch] = acc * _sigmoid(acc)

    dt = _softplus(dt_ref[...][:, :SSM_HEADS] + dtb_ref[...])
    a = dt * (-jnp.exp(alog_ref[...]))
    tril = jnp.where(causal, 1.0, 0.0).astype(BF16)
    cs = _dot(tril, jnp.concatenate(_split3(a), axis=1))
    acum = cs[:, :SSM_HEADS] + cs[:, SSM_HEADS:2 * SSM_HEADS] + cs[:, 2 * SSM_HEADS:]
    sel_r = lax.broadcasted_iota(jnp.int32, (SSM_HEADS, 3 * SSM_HEADS), 0)
    sel_c = lax.broadcasted_iota(jnp.int32, (SSM_HEADS, 3 * SSM_HEADS), 1)
    sel = jnp.where(sel_c % SSM_HEADS == sel_r, 1.0, 0.0).astype(BF16)
    acum_t = _dot_nt(sel, jnp.concatenate(_split3(acum), axis=1))
    a_last = acum[L - 1:L]
    fac = jnp.concatenate([dt, jnp.exp(acum), jnp.exp(a_last - acum)], axis=0)
    fac3 = jnp.concatenate(_split3(fac), axis=1)
    lane_head = lax.broadcasted_iota(jnp.int32, (L, GROUP_W), 1) // SSM_HEADDIM

    for g in range(SSM_GROUPS):
        lo, hi = g * GROUP_W, (g + 1) * GROUP_W
        xs_g = xc_ref[:, lo:hi]
        bm_g = xc_ref[:, D_INNER + g * D_STATE:D_INNER + (g + 1) * D_STATE].astype(BF16)
        cm_g = xc_ref[:, D_INNER + SSM_GROUPS * D_STATE + g * D_STATE:
                      D_INNER + SSM_GROUPS * D_STATE + (g + 1) * D_STATE].astype(BF16)
        fx = _dot(fac3, e3_ref[:, lo:hi])
        dt_x, ea_x, te_x = fx[:L], fx[L:2 * L], fx[2 * L:]
        xdt = xs_g * dt_x
        xdt_b = xdt.astype(BF16)
        cb = _dot_nt(cm_g, bm_g)
        ms = []
        for r in range(HEADS_PER_GROUP):
            h = g * HEADS_PER_GROUP + r
            seg = acum[:, h:h + 1] - acum_t[h:h + 1, :]
            dec = jnp.exp(jnp.where(causal, seg, -jnp.inf))
            ms.append((cb * dec).astype(BF16))
        yy = _dot(jnp.concatenate(ms, axis=0), xdt_b)
        y = yy[3 * L:]
        for r in range(HEADS_PER_GROUP - 2, -1, -1):
            y = jnp.where(lane_head == r, yy[r * L:(r + 1) * L], y)
        hc = hcat_ref[g]
        y = y + _dot(cm_g, hc.astype(BF16)) * ea_x
        hcat_ref[g] = ea_x[L - 1:L] * hc + _dot_tn(bm_g, (te_x * xdt).astype(BF16))
        y = y + dsk_ref[:, lo:hi] * xs_g
        zg = z_ref[:, lo:hi].astype(F32)
        y = y * (zg * _sigmoid(zg))
        y = y * lax.rsqrt(jnp.mean(y * y, axis=-1, keepdims=True) + EPS) * ng_ref[:, lo:hi]
        y_ref[:, lo:hi] = y.astype(y_ref.dtype)

    @pl.when(c == nc - 1)
    def _():
        for g in range(SSM_GROUPS):
            hout_ref[g] = hcat_ref[g].T


def _ssd(proj, dt_raw, hist8, h0, conv_w, conv_b, dtb, alog, dsk_x, norm_g, e3):
    b, t, _ = proj.shape
    L = min(t, 256)
    kern = functools.partial(_ssd_kernel, L=L)
    full = lambda shape: pl.BlockSpec(shape, lambda bi, ci: (0,) * len(shape))
    return pl.pallas_call(
        kern,
        grid=(b, t // L),
        in_specs=[pl.BlockSpec((None, L, CONV_DIM), lambda bi, ci: (bi, ci, COL_XBC // CONV_DIM)),
                  pl.BlockSpec((None, L, D_INNER), lambda bi, ci: (bi, ci, COL_Z // D_INNER)),
                  pl.BlockSpec((None, L, DT_PAD), lambda bi, ci: (bi, ci, 0)),
                  pl.BlockSpec((None, SUBLANE, CONV_DIM), lambda bi, ci: (bi, 0, 0)),
                  pl.BlockSpec((None, SSM_GROUPS, GROUP_W, D_STATE), lambda bi, ci: (bi, 0, 0, 0)),
                  full((CONV_W, CONV_DIM)), full((1, CONV_DIM)), full((1, SSM_HEADS)),
                  full((1, SSM_HEADS)), full((1, D_INNER)), full((1, D_INNER)),
                  full((3 * SSM_HEADS, D_INNER))],
        out_specs=[pl.BlockSpec((None, L, D_INNER), lambda bi, ci: (bi, ci, 0)),
                   pl.BlockSpec((None, SSM_GROUPS, GROUP_W, D_STATE), lambda bi, ci: (bi, 0, 0, 0))],
        out_shape=[jax.ShapeDtypeStruct((b, t, D_INNER), BF16),
                   jax.ShapeDtypeStruct((b, SSM_GROUPS, GROUP_W, D_STATE), F32)],
        scratch_shapes=[pltpu.VMEM((SSM_GROUPS, D_STATE, GROUP_W), F32),
                        pltpu.VMEM((SUBLANE, CONV_DIM), F32), pltpu.VMEM((L, CONV_DIM), F32)],
        compiler_params=_cparams(("arbitrary", "arbitrary")),
        name="ssd",
    )(proj, proj, dt_raw, hist8, h0, conv_w, conv_b, dtb, alog, dsk_x, norm_g, e3)


def _attn_kernel(sink_ref, q_ref, kp_ref, vp_ref, kc_ref, vc_ref, o_ref, *, tq, npast, banded):
    t = pl.program_id(1)
    cq = CHUNK if banded else tq
    nkw = npast + cq
    q = (q_ref[...] * ATTN_SCALE).astype(BF16)
    k = jnp.concatenate([kp_ref[...].astype(BF16), kc_ref[...].astype(BF16)], axis=0)
    v = jnp.concatenate([vp_ref[...].astype(BF16), vc_ref[...].astype(BF16)], axis=0)
    blocks = [(c, g) for c in range(tq // cq) for g in range(N_KV)]
    kj = lax.broadcasted_iota(jnp.int32, (1, nkw), 1)

    scores = []
    for c, g in blocks:
        qs = jnp.concatenate([q[c * cq:(c + 1) * cq, (g * Q_PER_KV + r) * HEAD_DIM:
                                (g * Q_PER_KV + r + 1) * HEAD_DIM] for r in range(Q_PER_KV)], axis=0)
        s = _dot_nt(qs, k[c * cq:c * cq + nkw, g * HEAD_DIM:(g + 1) * HEAD_DIM])
        if banded:
            s = s + jnp.where(t * tq + c * cq - npast + kj >= 0, 0.0, -jnp.inf)
        scores.append(s)
    probs = []
    for (c, g), s in zip(blocks, scores):
        sink = jnp.concatenate([jnp.full((cq, 1), sink_ref[g * Q_PER_KV + r], F32)
                                for r in range(Q_PER_KV)], axis=0)
        m = jnp.maximum(jnp.max(s, axis=-1, keepdims=True), sink)
        e = jnp.exp(s - m)
        den = jnp.sum(e, axis=-1, keepdims=True) + jnp.exp(sink - m)
        probs.append((e.astype(BF16), 1.0 / den))
    for (c, g), (e, inv) in zip(blocks, probs):
        o = _dot(e, v[c * cq:c * cq + nkw, g * HEAD_DIM:(g + 1) * HEAD_DIM]) * inv
        for r in range(Q_PER_KV):
            h = g * Q_PER_KV + r
            o_ref[c * cq:(c + 1) * cq, h * HEAD_DIM:(h + 1) * HEAD_DIM] = (
                o[r * cq:(r + 1) * cq].astype(o_ref.dtype))


def _attn(proj, k_hist, v_hist, sinks):
    b, t, _ = proj.shape
    banded = k_hist is None
    tq = min(t, 256)
    npast = WINDOW
    kw, kb, vb = KV_DIM, COL_K // KV_DIM, COL_V // KV_DIM
    if banded:
        blocks_per_tile = tq // npast
        prev = lambda col: pl.BlockSpec(
            (None, npast, kw), lambda bi, ti: (bi, jnp.maximum(ti * blocks_per_tile - 1, 0), col))
        kp_spec, vp_spec, kp, vp = prev(kb), prev(vb), proj, proj
    else:
        kp_spec = vp_spec = pl.BlockSpec((None, npast, kw), lambda bi, ti: (bi, 0, 0))
        kp, vp = k_hist, v_hist
    kern = functools.partial(_attn_kernel, tq=tq, npast=npast, banded=banded)
    return pl.pallas_call(
        kern,
        grid=(b, t // tq),
        in_specs=[pl.BlockSpec(memory_space=pltpu.SMEM),
                  pl.BlockSpec((None, tq, ATTN_DIM), lambda bi, ti: (bi, ti, COL_Q // ATTN_DIM)),
                  kp_spec, vp_spec,
                  pl.BlockSpec((None, tq, kw), lambda bi, ti: (bi, ti, kb)),
                  pl.BlockSpec((None, tq, kw), lambda bi, ti: (bi, ti, vb))],
        out_specs=pl.BlockSpec((None, tq, ATTN_DIM), lambda bi, ti: (bi, ti, 0)),
        out_shape=jax.ShapeDtypeStruct((b, t, ATTN_DIM), BF16),
        compiler_params=_cparams(("arbitrary", "arbitrary")),
        name="attn",
    )(sinks, proj, kp, vp, proj, proj)


def _merge_kernel(x_ref, y_ref, o_ref, gate_ref, mod_ref, g2_ref, wbs_ref, wba_ref, wo_ref,
                  x2_ref, h2_ref):
    d = x_ref.shape[-1]
    a = _dot(y_ref[...], wbs_ref[...])
    b = _dot(o_ref[...], wba_ref[...])
    gt = gate_ref[...].astype(F32)
    merged = _sigmoid(gt[:, :d]) * a + _sigmoid(gt[:, d:]) * b
    m = mod_ref[...]
    x2 = x_ref[...] + m[2:3] * _dot(merged.astype(BF16), wo_ref[...])
    x2_ref[...] = x2
    xn = x2 * lax.rsqrt(jnp.mean(x2 * x2, axis=-1, keepdims=True) + EPS) * g2_ref[...]
    h2_ref[...] = (xn * (1.0 + m[4:5]) + m[3:4]).astype(h2_ref.dtype)


def _merge(x, y_ssm, o_attn, proj, mod, g2, wbs, wba, wo):
    b, t, d = x.shape
    tm = min(t, 512)
    full = lambda shape: pl.BlockSpec(shape, lambda bi, i: (0,) * len(shape))
    return pl.pallas_call(
        _merge_kernel,
        grid=(b, t // tm),
        in_specs=[pl.BlockSpec((None, tm, d), lambda bi, i: (bi, i, 0)),
                  pl.BlockSpec((None, tm, D_INNER), lambda bi, i: (bi, i, 0)),
                  pl.BlockSpec((None, tm, ATTN_DIM), lambda bi, i: (bi, i, 0)),
                  pl.BlockSpec((None, tm, 2 * d), lambda bi, i: (bi, i, COL_GATE // (2 * d))),
                  pl.BlockSpec((None, 6, d), lambda bi, i: (bi, 0, 0)),
                  full((1, d)), full((D_INNER, d)), full((ATTN_DIM, d)), full((d, d))],
        out_specs=[pl.BlockSpec((None, tm, d), lambda bi, i: (bi, i, 0)),
                   pl.BlockSpec((None, tm, d), lambda bi, i: (bi, i, 0))],
        out_shape=[jax.ShapeDtypeStruct((b, t, d), F32), jax.ShapeDtypeStruct((b, t, d), BF16)],
        compiler_params=_cparams(("arbitrary", "arbitrary")),
        name="merge",
    )(x, y_ssm, o_attn, proj, mod, g2.reshape(1, d), wbs, wba, wo)


def _batcher_pairs(n):
    pairs = []
    p = 1
    while p < n:
        k = p
        while k >= 1:
            for j in range(k % p, n - k, 2 * k):
                for i in range(min(k, n - j - k)):
                    if (i + j) // (2 * p) == (i + j + k) // (2 * p):
                        pairs.append((i + j, i + j + k))
            k //= 2
        p *= 2
    return tuple(pairs)


_SORT16 = _batcher_pairs(PEER_TOPK)


def _max2(a, b):
    return b if a is None else a if b is None else jnp.maximum(a, b)


def _min2(a, b):
    return None if a is None or b is None else jnp.minimum(a, b)


def _cmpx(rows, i, j):
    rows[i], rows[j] = _max2(rows[i], rows[j]), _min2(rows[i], rows[j])


def _top17(rows):
    rows = list(rows)
    for i, j in _SORT16:
        _cmpx(rows, i, j)
    x17 = None
    for sh in (4, 2, 1):
        roll = lambda a: None if a is None else pltpu.roll(a, sh, axis=0)
        other = [roll(rows[PEER_TOPK - 1 - r]) for r in range(PEER_TOPK)]
        lo = [_min2(rows[r], other[r]) for r in range(PEER_TOPK)]
        rows = [_max2(rows[r], other[r]) for r in range(PEER_TOPK)]
        x17 = functools.reduce(_max2, lo + [x17, roll(x17)])
        for dd in (8, 4, 2, 1):
            for i in range(PEER_TOPK):
                if i & dd == 0:
                    _cmpx(rows, i, i + dd)
    return rows, x17


def _select_threshold(v1, x1, v2, x2):
    n = v1[0].shape[1]
    sub = lax.broadcasted_iota(jnp.int32, (SUBLANE, n), 0)

    def pack(vals):
        out = vals[SUBLANE - 1]
        for b in range(SUBLANE - 2, -1, -1):
            out = jnp.where(sub == b, vals[b], out)
        return out

    v2lo, v2hi, v1hi = pack(v2[:SUBLANE]), pack(v2[SUBLANE:]), pack(v1[SUBLANE:])
    cands = ([v1[0] + v2lo, v1[0] + v2hi] + [v1[a] + v2lo for a in range(1, SUBLANE)]
             + [v1hi + v2[0]])
    m0 = (v1[0] + v2[0])[0:1]
    zero = jnp.zeros((1, n), F32)
    cum, z = zero, zero
    c16 = jnp.full((1, n), -jnp.inf, F32)
    c17 = c16
    for _ in range(PEER_TOPK + 1):
        m = jnp.max(functools.reduce(jnp.maximum, cands), axis=0, keepdims=True)
        eqs = [c == m for c in cands]
        cnt = jnp.sum(functools.reduce(jnp.add, [jnp.where(q, 1.0, 0.0) for q in eqs]),
                      axis=0, keepdims=True)
        cands = [jnp.where(q, -jnp.inf, c) for q, c in zip(eqs, cands)]
        prev, cum = cum, cum + cnt
        z = z + jnp.minimum(cnt, jnp.maximum(PEER_TOPK - prev, 0.0)) * jnp.exp(m - m0)
        c16 = jnp.where((prev < PEER_TOPK) & (cum >= PEER_TOPK), m, c16)
        c17 = jnp.where((prev < PEER_TOPK + 1) & (cum >= PEER_TOPK + 1), m, c17)
    c17 = jnp.maximum(c17, jnp.maximum(x1 + v2[0], v1[0] + x2)[0:1])
    return 0.5 * (c16 + c17), z


GELU_C0 = 0.7978845608028654
GELU_C1 = GELU_C0 * 0.044715


def _gelu_tanh_x2(x):
    return x * (1.0 + jnp.tanh(x * (GELU_C0 + GELU_C1 * (x * x))))


PEER_EB = 2048
PEER_SB = 512


def _peer_kernel(h2_ref, wqt_ref, keys_ref, u_ref, un_ref, vt_ref, vtp_ref, o_ref,
                 h2t_ref, s2_ref, thr_ref, aiw_ref, bj_ref, acc_ref, s_first_ref, coef_last_ref):
    e = pl.program_id(1)
    ne = pl.num_programs(1)
    tb = h2_ref.shape[0]

    @pl.when(e == 0)
    def _():
        h2t_ref[...] = h2_ref[...].astype(F32).T.astype(BF16)
        qt = _dot(wqt_ref[...], h2t_ref[...])
        for h in range(PEER_HEADS):
            q1 = qt[(2 * h) * D_HALF:(2 * h + 1) * D_HALF].astype(BF16)
            q2 = qt[(2 * h + 1) * D_HALF:(2 * h + 2) * D_HALF].astype(BF16)
            s1 = _dot(keys_ref[2 * h], q1)
            s2 = _dot(keys_ref[2 * h + 1], q2)
            v1, x1 = _top17([s1[r * SUBLANE:(r + 1) * SUBLANE] for r in range(PEER_TOPK)])
            v2, x2 = _top17([s2[r * SUBLANE:(r + 1) * SUBLANE] for r in range(PEER_TOPK)])
            tau, z = _select_threshold(v1, x1, v2, x2)
            s2_ref[h] = s2
            thr_ref[h] = tau - s1
            ai = jnp.exp(s1 - v1[0][0:1]).astype(BF16).astype(F32)
            bits = pltpu.bitcast(ai, jnp.uint32)
            aiw_ref[h] = bits | (bits >> 16)
            bj_ref[h] = (jnp.exp(s2 - v2[0][0:1]) * (0.5 / z)).astype(BF16)
        acc_ref[...] = jnp.zeros_like(acc_ref)
        coef_last_ref[...] = jnp.zeros_like(coef_last_ref)
        s_first_ref[...] = _dot(u_ref[0:PEER_SB, :], h2t_ref[...])

    nsb = PEER_EB // PEER_SB
    nib = PEER_SB // N_KEYS
    halves = [(0, tb)] if tb < 4 * LANE else [(0, tb // 2), (tb // 2, tb)]

    def pre_acts(k, lo, hi):
        rows = un_ref[...] if k == nsb else u_ref[k * PEER_SB:(k + 1) * PEER_SB, :]
        return _dot(rows, h2t_ref[:, lo:hi])

    def coefficients(k, ii, s):
        i = e * (PEER_EB // N_KEYS) + k * nib + ii
        act = _gelu_tanh_x2(s[ii * N_KEYS:(ii + 1) * N_KEYS]).astype(BF16)
        w = None
        for h in range(PEER_HEADS):
            hit = s2_ref[h] >= thr_ref[h, pl.ds(i, 1), :]
            ai = pltpu.bitcast(jnp.broadcast_to(aiw_ref[h, pl.ds(i, 1), :], (SUBLANE, tb)), BF16)
            ai = jnp.concatenate([ai] * (N_KEYS // (2 * SUBLANE)), axis=0)
            wh = jnp.where(hit, ai * bj_ref[h], jnp.zeros((), BF16))
            w = wh if w is None else w + wh
        return w * act

    def accumulate(vt_sub, coef, lo, hi):
        acc_ref[:, lo:hi] += _dot(vt_sub, coef[:, lo:hi])

    s_cur = s_first_ref[...]
    coef_prev = coef_last_ref[...]
    for k in range(nsb):
        vt_prev = vtp_ref[...] if k == 0 else vt_ref[:, (k - 1) * PEER_SB:k * PEER_SB]
        tasks = [(kind, lo, hi) for lo, hi in halves for kind in ("pre", "acc")]
        s_parts, coef = [], []
        for ii in range(nib):
            for kind, lo, hi in tasks[ii * len(tasks) // nib:(ii + 1) * len(tasks) // nib]:
                if kind == "pre":
                    s_parts.append(pre_acts(k + 1, lo, hi))
                else:
                    accumulate(vt_prev, coef_prev, lo, hi)
            coef.append(coefficients(k, ii, s_cur))
        s_cur = jnp.concatenate(s_parts, axis=1)
        coef_prev = jnp.concatenate(coef, axis=0)
    s_first_ref[...] = s_cur
    coef_last_ref[...] = coef_prev

    @pl.when(e == ne - 1)
    def _():
        accumulate(vt_ref[:, (nsb - 1) * PEER_SB:], coef_last_ref[...], 0, tb)
        o_ref[...] = acc_ref[...].T


def _peer(h2, wqt, keys, u, vt):
    n, d = h2.shape
    tb = min(n, 512)
    hs = (PEER_HEADS, N_KEYS, tb)
    ne, nsb = N_EXPERTS // PEER_EB, PEER_EB // PEER_SB
    return pl.pallas_call(
        _peer_kernel,
        grid=(n // tb, ne),
        in_specs=[pl.BlockSpec((tb, d), lambda i, e: (i, 0)),
                  pl.BlockSpec(wqt.shape, lambda i, e: (0, 0)),
                  pl.BlockSpec(keys.shape, lambda i, e: (0, 0, 0)),
                  pl.BlockSpec((PEER_EB, d), lambda i, e: (e, 0)),
                  pl.BlockSpec((PEER_SB, d), lambda i, e: (jnp.minimum(e + 1, ne - 1) * nsb, 0)),
                  pl.BlockSpec((d, PEER_EB), lambda i, e: (0, e)),
                  pl.BlockSpec((d, PEER_SB), lambda i, e: (0, jnp.maximum(e * nsb - 1, 0)))],
        out_specs=pl.BlockSpec((tb, d), lambda i, e: (i, 0)),
        out_shape=jax.ShapeDtypeStruct((n, d), F32),
        scratch_shapes=[pltpu.VMEM((d, tb), BF16), pltpu.VMEM(hs, F32), pltpu.VMEM(hs, F32),
                        pltpu.VMEM(hs, jnp.uint32), pltpu.VMEM(hs, BF16), pltpu.VMEM((d, tb), F32),
                        pltpu.VMEM((PEER_SB, tb), F32), pltpu.VMEM((PEER_SB, tb), BF16)],
        compiler_params=_cparams(("arbitrary", "arbitrary")),
        name="peer",
    )(h2, wqt, keys, u, u, vt, vt)


def _final_kernel(x2_ref, p_ref, mod_ref, g_ref, o_ref, *, normalize):
    x = x2_ref[...] + mod_ref[...][5:6] * p_ref[...]
    if normalize:
        x = x * lax.rsqrt(jnp.mean(x * x, axis=-1, keepdims=True) + EPS) * g_ref[...]
    o_ref[...] = x


def _final(x2, peer_out, mod, g, normalize):
    b, t, d = x2.shape
    tm = min(t, 1024)
    spec = pl.BlockSpec((None, tm, d), lambda bi, i: (bi, i, 0))
    return pl.pallas_call(
        functools.partial(_final_kernel, normalize=normalize),
        grid=(b, t // tm),
        in_specs=[spec, spec,
                  pl.BlockSpec((None, 6, d), lambda bi, i: (bi, 0, 0)),
                  pl.BlockSpec((1, d), lambda bi, i: (0, 0))],
        out_specs=spec,
        out_shape=jax.ShapeDtypeStruct((b, t, d), F32),
        compiler_params=_cparams(("arbitrary", "arbitrary")),
        name="final_norm",
    )(x2, peer_out, mod, g.reshape(1, d))


def _prep_layer_weights(w_in, conv_w, conv_b, dt_bias, a_log, d_skip, ssm_norm_g, w_branch_ssm,
                        w_branch_attn, w_out, peer_wq, peer_keys, peer_u, peer_v):
    d = w_in.shape[0]
    off_xbc = D_INNER
    off_dt = off_xbc + CONV_DIM
    off_q = off_dt + SSM_HEADS
    off_k = off_q + ATTN_DIM
    off_v = off_k + KV_DIM
    off_gate = off_v + KV_DIM
    w_re = jnp.concatenate([
        w_in[:, off_xbc:off_dt], w_in[:, :off_xbc], w_in[:, off_gate:], w_in[:, off_q:off_k],
        w_in[:, off_k:off_v], w_in[:, off_v:off_gate]], axis=1).astype(BF16)
    w_dt = jnp.pad(w_in[:, off_dt:off_q], ((0, 0), (0, DT_PAD - SSM_HEADS))).astype(BF16)
    head_of_lane = jnp.arange(D_INNER) // SSM_HEADDIM
    e1 = (jnp.arange(SSM_HEADS)[:, None] == head_of_lane[None, :]).astype(BF16)
    return dict(
        w_re=w_re, w_dt=w_dt, conv_w=conv_w, conv_b=conv_b.reshape(1, CONV_DIM),
        dtb=dt_bias.reshape(1, SSM_HEADS), alog=a_log.reshape(1, SSM_HEADS),
        dsk_x=jnp.repeat(d_skip, SSM_HEADDIM).reshape(1, D_INNER),
        norm_g=ssm_norm_g.reshape(1, D_INNER), e3=jnp.concatenate([e1, e1, e1], axis=0),
        wbs=w_branch_ssm.astype(BF16), wba=w_branch_attn.astype(BF16), wo=w_out.astype(BF16),
        wqt=peer_wq.T.astype(BF16),
        keys=peer_keys.reshape(PEER_HEADS * 2, N_KEYS, D_HALF).astype(BF16),
        u=peer_u.astype(BF16), vt=peer_v.T.astype(BF16))


def _layer(x, mod, conv_hist, ssm0, k_hist, v_hist, norm1_g, norm2_g, sinks, lw):
    b, t, d = x.shape
    proj, dt_raw = _proj(x, mod, norm1_g, lw["w_re"], lw["w_dt"])
    hist8 = jnp.pad(conv_hist, ((0, 0), (SUBLANE - (CONV_W - 1), 0), (0, 0)))
    h0 = ssm0.reshape(b, SSM_GROUPS, GROUP_W, D_STATE)
    y_ssm, h_new = _ssd(proj, dt_raw, hist8, h0, lw["conv_w"], lw["conv_b"], lw["dtb"], lw["alog"],
                        lw["dsk_x"], lw["norm_g"], lw["e3"])
    if k_hist is not None:
        k_hist = k_hist.reshape(b, -1, KV_DIM)
        v_hist = v_hist.reshape(b, -1, KV_DIM)
    o_attn = _attn(proj, k_hist, v_hist, sinks)
    x2, h2 = _merge(x, y_ssm, o_attn, proj, mod, norm2_g, lw["wbs"], lw["wba"], lw["wo"])
    peer_out = _peer(h2.reshape(b * t, d), lw["wqt"], lw["keys"], lw["u"], lw["vt"]).reshape(b, t, d)
    keep = WINDOW if k_hist is None else t
    k_new = proj[:, t - keep:, COL_K:COL_K + KV_DIM].astype(F32).reshape(b, keep, N_KV, HEAD_DIM)
    v_new = proj[:, t - keep:, COL_V:COL_V + KV_DIM].astype(F32).reshape(b, keep, N_KV, HEAD_DIM)
    conv_new = proj[:, t - (CONV_W - 1):, COL_XBC:COL_XBC + CONV_DIM].astype(F32)
    ssm_new = h_new.reshape(b, SSM_HEADS, SSM_HEADDIM, D_STATE)
    return x2, peer_out, k_new, v_new, conv_new, ssm_new


def kernel(x_prompt, x_sample, cache_attn_k, cache_attn_v, state_conv, state_ssm, c_prompt, c_sample, ada_w, ada_b, norm1_g, w_in, conv_w, conv_b, dt_bias, a_log, d_skip, ssm_norm_g, attn_sinks, w_branch_ssm, w_branch_attn, w_out, norm2_g, peer_wq, peer_keys, peer_u, peer_v, final_g):
    depth = ada_w.shape[0]
    d = x_prompt.shape[-1]
    bp, bs = x_prompt.shape[0], x_sample.shape[0]
    xp, xs = x_prompt, x_sample
    c_all = jnp.concatenate([c_prompt, c_sample], axis=0)
    outs = [[] for _ in range(8)]
    for l in range(depth):
        lw = _prep_layer_weights(w_in[l], conv_w[l], conv_b[l], dt_bias[l], a_log[l], d_skip[l],
                                 ssm_norm_g[l], w_branch_ssm[l], w_branch_attn[l], w_out[l],
                                 peer_wq[l], peer_keys[l], peer_u[l], peer_v[l])
        mod = _adaln(c_all, ada_w[l], ada_b[l]).reshape(bp + bs, 6, d)
        last = l == depth - 1
        conv0 = jnp.zeros((bp, CONV_W - 1, CONV_DIM), F32)
        ssm0 = jnp.zeros((bp, SSM_HEADS, SSM_HEADDIM, D_STATE), F32)
        res = []
        for x, m, ch, s0, kh, vh in ((xp, mod[:bp], conv0, ssm0, None, None),
                                     (xs, mod[bp:], state_conv[l], state_ssm[l],
                                      cache_attn_k[l], cache_attn_v[l])):
            x2, peer_out, kn, vn, cn, sn = _layer(x, m, ch, s0, kh, vh, norm1_g[l], norm2_g[l],
                                                  attn_sinks[l], lw)
            res.append((_final(x2, peer_out, m, final_g, last), kn, vn, cn, sn))
        (xp, kp, vp, cp, sp), (xs, kn, vn, cn, sn) = res
        for lst, val in zip(outs, (kp, vp, cp, sp, kn, vn, cn, sn)):
            lst.append(val)
    return (xp, xs) + tuple(jnp.stack(o) for o in outs)
```

```python
import functools

import jax
import jax.numpy as jnp
from jax import lax
from jax.experimental import pallas as pl
from jax.experimental.pallas import tpu as pltpu

F32 = jnp.float32
BF16 = jnp.bfloat16

EPS = 1e-6
CHUNK = 64
WINDOW = 128
SSM_HEADS = 32
SSM_HEADDIM = 64
D_STATE = 128
SSM_GROUPS = 8
HEADS_PER_GROUP = SSM_HEADS // SSM_GROUPS
GROUP_W = HEADS_PER_GROUP * SSM_HEADDIM
D_INNER = SSM_HEADS * SSM_HEADDIM
CONV_W = 4
CONV_DIM = D_INNER + 2 * SSM_GROUPS * D_STATE
N_HEADS = 16
N_KV = 4
HEAD_DIM = 64
Q_PER_KV = N_HEADS // N_KV
ATTN_DIM = N_HEADS * HEAD_DIM
KV_DIM = N_KV * HEAD_DIM
ATTN_SCALE = HEAD_DIM ** -0.5
PEER_HEADS = 8
N_KEYS = 128
N_EXPERTS = N_KEYS * N_KEYS
PEER_TOPK = 16
D_HALF = 128

LANE = 128
SUBLANE = 8
DT_PAD = LANE

COL_XBC = 0
COL_Z = COL_XBC + CONV_DIM
COL_GATE = COL_Z + D_INNER
COL_Q = COL_GATE + 2048
COL_K = COL_Q + ATTN_DIM
COL_V = COL_K + KV_DIM
PROJ_W = COL_V + KV_DIM
PROJ_TN = 2432
CONV_CB = 1024

VMEM_LIMIT = 56 * 1024 * 1024


def _cparams(sem):
    return pltpu.CompilerParams(dimension_semantics=sem, vmem_limit_bytes=VMEM_LIMIT)


def _sigmoid(x):
    return 1.0 / (1.0 + jnp.exp(-x))


def _split2(v):
    hi = v.astype(BF16)
    lo = (v - hi.astype(F32)).astype(BF16)
    return hi, lo


def _split3(v):
    hi = v.astype(BF16)
    r = v - hi.astype(F32)
    mid = r.astype(BF16)
    lo = (r - mid.astype(F32)).astype(BF16)
    return hi, mid, lo


def _dot(a, b):
    return jnp.dot(a, b, preferred_element_type=F32)


def _dot_nt(a, b):
    return lax.dot_general(a, b, (((1,), (1,)), ((), ())), preferred_element_type=F32)


def _dot_tn(a, b):
    return lax.dot_general(a, b, (((0,), (0,)), ((), ())), preferred_element_type=F32)


def _adaln_kernel(c_ref, w_ref, b_ref, o_ref):
    c = c_ref[...]
    s = c * _sigmoid(c)
    s_hi, s_lo = _split2(s)
    w_hi, w_lo = _split2(w_ref[...])
    acc = _dot(s_hi, w_hi) + _dot(s_hi, w_lo) + _dot(s_lo, w_hi)
    o_ref[...] = acc + b_ref[...]


def _adaln(c, ada_w, ada_b):
    rows, d = c.shape
    n = ada_w.shape[1]
    tn = 1024
    return pl.pallas_call(
        _adaln_kernel,
        grid=(n // tn,),
        in_specs=[pl.BlockSpec((rows, d), lambda j: (0, 0)),
                  pl.BlockSpec((d, tn), lambda j: (0, j)),
                  pl.BlockSpec((1, tn), lambda j: (0, j))],
        out_specs=pl.BlockSpec((rows, tn), lambda j: (0, j)),
        out_shape=jax.ShapeDtypeStruct((rows, n), F32),
        compiler_params=_cparams(("arbitrary",)),
        name="adaln",
    )(c, ada_w, ada_b.reshape(1, n))


def _proj_kernel(x_ref, mod_ref, g_ref, w_ref, wdt_ref, o_ref, dt_ref, h_ref):
    @pl.when(pl.program_id(2) == 0)
    def _():
        x = x_ref[...]
        ms = jnp.mean(x * x, axis=-1, keepdims=True)
        xn = x * lax.rsqrt(ms + EPS) * g_ref[...]
        m = mod_ref[...]
        h_ref[...] = (xn * (1.0 + m[1:2]) + m[0:1]).astype(BF16)
        dt_ref[...] = _dot(h_ref[...], wdt_ref[...])

    o_ref[...] = _dot(h_ref[...], w_ref[...]).astype(o_ref.dtype)


def _proj(x, mod, g, w, w_dt):
    b, t, d = x.shape
    tm = min(t, 1024)
    return pl.pallas_call(
        _proj_kernel,
        grid=(b, t // tm, PROJ_W // PROJ_TN),
        in_specs=[pl.BlockSpec((None, tm, d), lambda bi, i, j: (bi, i, 0)),
                  pl.BlockSpec((None, 6, d), lambda bi, i, j: (bi, 0, 0)),
                  pl.BlockSpec((1, d), lambda bi, i, j: (0, 0)),
                  pl.BlockSpec((d, PROJ_TN), lambda bi, i, j: (0, j)),
                  pl.BlockSpec((d, DT_PAD), lambda bi, i, j: (0, 0))],
        out_specs=[pl.BlockSpec((None, tm, PROJ_TN), lambda bi, i, j: (bi, i, j)),
                   pl.BlockSpec((None, tm, DT_PAD), lambda bi, i, j: (bi, i, 0))],
        out_shape=[jax.ShapeDtypeStruct((b, t, PROJ_W), BF16),
                   jax.ShapeDtypeStruct((b, t, DT_PAD), F32)],
        scratch_shapes=[pltpu.VMEM((tm, d), BF16)],
        compiler_params=_cparams(("arbitrary", "arbitrary", "arbitrary")),
        name="norm_proj",
    )(x, mod, g.reshape(1, d), w, w_dt)


def _softplus(x):
    return jnp.maximum(x, 0.0) + jnp.log1p(jnp.exp(-jnp.abs(x)))


def _ssd_kernel(xbc_ref, z_ref, dt_ref, hist_ref, h0_ref, cw_ref, cb_ref, dtb_ref, alog_ref,
                dsk_ref, ng_ref, e3_ref, y_ref, hout_ref, hcat_ref, tail_ref, xc_ref, *, L):
    c = pl.program_id(1)
    nc = pl.num_programs(1)

    @pl.when(c == 0)
    def _():
        tail_ref[...] = hist_ref[...]
        for g in range(SSM_GROUPS):
            hcat_ref[g] = h0_ref[g].T

    row = lax.broadcasted_iota(jnp.int32, (L, L), 0)
    col = lax.broadcasted_iota(jnp.int32, (L, L), 1)
    causal = row >= col

    shifts = jnp.concatenate([jnp.where(row - col == k, 1.0, 0.0) for k in range(1, CONV_W)],
                             axis=0).astype(BF16)
    hp = tail_ref[...]
    row8 = lax.broadcasted_iota(jnp.int32, (SUBLANE, CONV_CB), 0)
    for cl in range(0, CONV_DIM, CONV_CB):
        ch = cl + CONV_CB
        xb = xbc_ref[:, cl:ch]
        x = xb.astype(F32)
        sh = _dot(shifts, xb)
        acc = cb_ref[:, cl:ch] + x * cw_ref[CONV_W - 1:CONV_W, cl:ch]
        for k in range(1, CONV_W):
            xr = sh[(k - 1) * L:k * L]
            top = jnp.where(row8 < k, pltpu.roll(hp[:, cl:ch], k, axis=0), xr[:SUBLANE])
            xr = jnp.concatenate([top, xr[SUBLANE:]], axis=0)
            acc = acc + xr * cw_ref[CONV_W - 1 - k:CONV_W - k, cl:ch]
        tail_ref[:, cl:ch] = x[L - SUBLANE:]
        xc_ref[:, cl:ch] = acc * _sigmoid(acc)

    dt = _softplus(dt_ref[...][:, :SSM_HEADS] + dtb_ref[...])
    a = dt * (-jnp.exp(alog_ref[...]))
    tril = jnp.where(causal, 1.0, 0.0).astype(BF16)
    cs = _dot(tril, jnp.concatenate(_split3(a), axis=1))
    acum = cs[:, :SSM_HEADS] + cs[:, SSM_HEADS:2 * SSM_HEADS] + cs[:, 2 * SSM_HEADS:]
    sel_r = lax.broadcasted_iota(jnp.int32, (SSM_HEADS, 3 * SSM_HEADS), 0)
    sel_c = lax.broadcasted_iota(jnp.int32, (SSM_HEADS, 3 * SSM_HEADS), 1)
    sel = jnp.where(sel_c % SSM_HEADS == sel_r, 1.0, 0.0).astype(BF16)
    acum_t = _dot_nt(sel, jnp.concatenate(_split3(acum), axis=1))
    a_last = acum[L - 1:L]
    fac = jnp.concatenate([dt, jnp.exp(acum), jnp.exp(a_last - acum)], axis=0)
    fac3 = jnp.concatenate(_split3(fac), axis=1)
    lane_head = lax.broadcasted_iota(jnp.int32, (L, GROUP_W), 1) // SSM_HEADDIM

    for g in range(SSM_GROUPS):
        lo, hi = g * GROUP_W, (g + 1) * GROUP_W
        xs_g = xc_ref[:, lo:hi]
        bm_g = xc_ref[:, D_INNER + g * D_STATE:D_INNER + (g + 1) * D_STATE].astype(BF16)
        cm_g = xc_ref[:, D_INNER + SSM_GROUPS * D_STATE + g * D_STATE:
                      D_INNER + SSM_GROUPS * D_STATE + (g + 1) * D_STATE].astype(BF16)
        fx = _dot(fac3, e3_ref[:, lo:hi])
        dt_x, ea_x, te_x = fx[:L], fx[L:2 * L], fx[2 * L:]
        xdt = xs_g * dt_x
        xdt_b = xdt.astype(BF16)
        cb = _dot_nt(cm_g, bm_g)
        ms = []
        for r in range(HEADS_PER_GROUP):
            h = g * HEADS_PER_GROUP + r
            seg = acum[:, h:h + 1] - acum_t[h:h + 1, :]
            dec = jnp.exp(jnp.where(causal, seg, -jnp.inf))
            ms.append((cb * dec).astype(BF16))
        yy = _dot(jnp.concatenate(ms, axis=0), xdt_b)
        y = yy[3 * L:]
        for r in range(HEADS_PER_GROUP - 2, -1, -1):
            y = jnp.where(lane_head == r, yy[r * L:(r + 1) * L], y)
        hc = hcat_ref[g]
        y = y + _dot(cm_g, hc.astype(BF16)) * ea_x
        hcat_ref[g] = ea_x[L - 1:L] * hc + _dot_tn(bm_g, (te_x * xdt).astype(BF16))
        y = y + dsk_ref[:, lo:hi] * xs_g
        zg = z_ref[:, lo:hi].astype(F32)
        y = y * (zg * _sigmoid(zg))
        y = y * lax.rsqrt(jnp.mean(y * y, axis=-1, keepdims=True) + EPS) * ng_ref[:, lo:hi]
        y_ref[:, lo:hi] = y.astype(y_ref.dtype)

    @pl.when(c == nc - 1)
    def _():
        for g in range(SSM_GROUPS):
            hout_ref[g] = hcat_ref[g].T


def _ssd(proj, dt_raw, hist8, h0, conv_w, conv_b, dtb, alog, dsk_x, norm_g, e3):
    b, t, _ = proj.shape
    L = min(t, 256)
    kern = functools.partial(_ssd_kernel, L=L)
    full = lambda shape: pl.BlockSpec(shape, lambda bi, ci: (0,) * len(shape))
    return pl.pallas_call(
        kern,
        grid=(b, t // L),
        in_specs=[pl.BlockSpec((None, L, CONV_DIM), lambda bi, ci: (bi, ci, COL_XBC // CONV_DIM)),
                  pl.BlockSpec((None, L, D_INNER), lambda bi, ci: (bi, ci, COL_Z // D_INNER)),
                  pl.BlockSpec((None, L, DT_PAD), lambda bi, ci: (bi, ci, 0)),
                  pl.BlockSpec((None, SUBLANE, CONV_DIM), lambda bi, ci: (bi, 0, 0)),
                  pl.BlockSpec((None, SSM_GROUPS, GROUP_W, D_STATE), lambda bi, ci: (bi, 0, 0, 0)),
                  full((CONV_W, CONV_DIM)), full((1, CONV_DIM)), full((1, SSM_HEADS)),
                  full((1, SSM_HEADS)), full((1, D_INNER)), full((1, D_INNER)),
                  full((3 * SSM_HEADS, D_INNER))],
        out_specs=[pl.BlockSpec((None, L, D_INNER), lambda bi, ci: (bi, ci, 0)),
                   pl.BlockSpec((None, SSM_GROUPS, GROUP_W, D_STATE), lambda bi, ci: (bi, 0, 0, 0))],
        out_shape=[jax.ShapeDtypeStruct((b, t, D_INNER), BF16),
                   jax.ShapeDtypeStruct((b, SSM_GROUPS, GROUP_W, D_STATE), F32)],
        scratch_shapes=[pltpu.VMEM((SSM_GROUPS, D_STATE, GROUP_W), F32),
                        pltpu.VMEM((SUBLANE, CONV_DIM), F32), pltpu.VMEM((L, CONV_DIM), F32)],
        compiler_params=_cparams(("arbitrary", "arbitrary")),
        name="ssd",
    )(proj, proj, dt_raw, hist8, h0, conv_w, conv_b, dtb, alog, dsk_x, norm_g, e3)


def _attn_kernel(sink_ref, q_ref, kp_ref, vp_ref, kc_ref, vc_ref, o_ref, *, tq, npast, banded):
    t = pl.program_id(1)
    cq = CHUNK if banded else tq
    nkw = npast + cq
    q = (q_ref[...] * ATTN_SCALE).astype(BF16)
    k = jnp.concatenate([kp_ref[...].astype(BF16), kc_ref[...].astype(BF16)], axis=0)
    v = jnp.concatenate([vp_ref[...].astype(BF16), vc_ref[...].astype(BF16)], axis=0)
    blocks = [(c, g) for c in range(tq // cq) for g in range(N_KV)]
    kj = lax.broadcasted_iota(jnp.int32, (1, nkw), 1)

    scores = []
    for c, g in blocks:
        qs = jnp.concatenate([q[c * cq:(c + 1) * cq, (g * Q_PER_KV + r) * HEAD_DIM:
                                (g * Q_PER_KV + r + 1) * HEAD_DIM] for r in range(Q_PER_KV)], axis=0)
        s = _dot_nt(qs, k[c * cq:c * cq + nkw, g * HEAD_DIM:(g + 1) * HEAD_DIM])
        if banded:
            s = s + jnp.where(t * tq + c * cq - npast + kj >= 0, 0.0, -jnp.inf)
        scores.append(s)
    sinks = [jnp.concatenate([jnp.full((cq, 1), sink_ref[g * Q_PER_KV + r], F32)
                              for r in range(Q_PER_KV)], axis=0) for g in range(N_KV)]
    ms = [jnp.maximum(jnp.max(s, axis=-1, keepdims=True), sinks[g]) for (c, g), s in zip(blocks, scores)]
    es = [jnp.exp(s - m) for s, m in zip(scores, ms)]
    dens = [jnp.sum(e, axis=-1, keepdims=True) for e in es]
    invs = [1.0 / (den + jnp.exp(sinks[g] - m)) for (c, g), den, m in zip(blocks, dens, ms)]
    outs = [_dot(e.astype(BF16), v[c * cq:c * cq + nkw, g * HEAD_DIM:(g + 1) * HEAD_DIM])
            for (c, g), e in zip(blocks, es)]
    outs = [o * inv for o, inv in zip(outs, invs)]
    for (c, g), o in zip(blocks, outs):
        for r in range(Q_PER_KV):
            h = g * Q_PER_KV + r
            o_ref[c * cq:(c + 1) * cq, h * HEAD_DIM:(h + 1) * HEAD_DIM] = (
                o[r * cq:(r + 1) * cq].astype(o_ref.dtype))


def _attn(proj, k_hist, v_hist, sinks):
    b, t, _ = proj.shape
    banded = k_hist is None
    tq = min(t, 256)
    npast = WINDOW
    kw, kb, vb = KV_DIM, COL_K // KV_DIM, COL_V // KV_DIM
    if banded:
        blocks_per_tile = tq // npast
        prev = lambda col: pl.BlockSpec(
            (None, npast, kw), lambda bi, ti: (bi, jnp.maximum(ti * blocks_per_tile - 1, 0), col))
        kp_spec, vp_spec, kp, vp = prev(kb), prev(vb), proj, proj
    else:
        kp_spec = vp_spec = pl.BlockSpec((None, npast, kw), lambda bi, ti: (bi, 0, 0))
        kp, vp = k_hist, v_hist
    kern = functools.partial(_attn_kernel, tq=tq, npast=npast, banded=banded)
    return pl.pallas_call(
        kern,
        grid=(b, t // tq),
        in_specs=[pl.BlockSpec(memory_space=pltpu.SMEM),
                  pl.BlockSpec((None, tq, ATTN_DIM), lambda bi, ti: (bi, ti, COL_Q // ATTN_DIM)),
                  kp_spec, vp_spec,
                  pl.BlockSpec((None, tq, kw), lambda bi, ti: (bi, ti, kb)),
                  pl.BlockSpec((None, tq, kw), lambda bi, ti: (bi, ti, vb))],
        out_specs=pl.BlockSpec((None, tq, ATTN_DIM), lambda bi, ti: (bi, ti, 0)),
        out_shape=jax.ShapeDtypeStruct((b, t, ATTN_DIM), BF16),
        compiler_params=_cparams(("arbitrary", "arbitrary")),
        name="attn",
    )(sinks, proj, kp, vp, proj, proj)


def _merge_kernel(x_ref, y_ref, o_ref, gate_ref, mod_ref, g2_ref, wbs_ref, wba_ref, wo_ref,
                  x2_ref, h2_ref):
    d = x_ref.shape[-1]
    a = _dot(y_ref[...], wbs_ref[...])
    b = _dot(o_ref[...], wba_ref[...])
    gt = gate_ref[...].astype(F32)
    merged = _sigmoid(gt[:, :d]) * a + _sigmoid(gt[:, d:]) * b
    m = mod_ref[...]
    x2 = x_ref[...] + m[2:3] * _dot(merged.astype(BF16), wo_ref[...])
    x2_ref[...] = x2
    xn = x2 * lax.rsqrt(jnp.mean(x2 * x2, axis=-1, keepdims=True) + EPS) * g2_ref[...]
    h2_ref[...] = (xn * (1.0 + m[4:5]) + m[3:4]).astype(h2_ref.dtype)


def _merge(x, y_ssm, o_attn, proj, mod, g2, wbs, wba, wo):
    b, t, d = x.shape
    tm = min(t, 512)
    full = lambda shape: pl.BlockSpec(shape, lambda bi, i: (0,) * len(shape))
    return pl.pallas_call(
        _merge_kernel,
        grid=(b, t // tm),
        in_specs=[pl.BlockSpec((None, tm, d), lambda bi, i: (bi, i, 0)),
                  pl.BlockSpec((None, tm, D_INNER), lambda bi, i: (bi, i, 0)),
                  pl.BlockSpec((None, tm, ATTN_DIM), lambda bi, i: (bi, i, 0)),
                  pl.BlockSpec((None, tm, 2 * d), lambda bi, i: (bi, i, COL_GATE // (2 * d))),
                  pl.BlockSpec((None, 6, d), lambda bi, i: (bi, 0, 0)),
                  full((1, d)), full((D_INNER, d)), full((ATTN_DIM, d)), full((d, d))],
        out_specs=[pl.BlockSpec((None, tm, d), lambda bi, i: (bi, i, 0)),
                   pl.BlockSpec((None, tm, d), lambda bi, i: (bi, i, 0))],
        out_shape=[jax.ShapeDtypeStruct((b, t, d), F32), jax.ShapeDtypeStruct((b, t, d), BF16)],
        compiler_params=_cparams(("arbitrary", "arbitrary")),
        name="merge",
    )(x, y_ssm, o_attn, proj, mod, g2.reshape(1, d), wbs, wba, wo)


def _batcher_pairs(n):
    pairs = []
    p = 1
    while p < n:
        k = p
        while k >= 1:
            for j in range(k % p, n - k, 2 * k):
                for i in range(min(k, n - j - k)):
                    if (i + j) // (2 * p) == (i + j + k) // (2 * p):
                        pairs.append((i + j, i + j + k))
            k //= 2
        p *= 2
    return tuple(pairs)


_SORT16 = _batcher_pairs(PEER_TOPK)


def _max2(a, b):
    return b if a is None else a if b is None else jnp.maximum(a, b)


def _min2(a, b):
    return None if a is None or b is None else jnp.minimum(a, b)


def _cmpx(rows, i, j):
    rows[i], rows[j] = _max2(rows[i], rows[j]), _min2(rows[i], rows[j])


def _top17(rows):
    rows = list(rows)
    for i, j in _SORT16:
        _cmpx(rows, i, j)
    x17 = None
    for sh in (4, 2, 1):
        roll = lambda a: None if a is None else pltpu.roll(a, sh, axis=0)
        other = [roll(rows[PEER_TOPK - 1 - r]) for r in range(PEER_TOPK)]
        lo = [_min2(rows[r], other[r]) for r in range(PEER_TOPK)]
        rows = [_max2(rows[r], other[r]) for r in range(PEER_TOPK)]
        x17 = functools.reduce(_max2, lo + [x17, roll(x17)])
        for dd in (8, 4, 2, 1):
            for i in range(PEER_TOPK):
                if i & dd == 0:
                    _cmpx(rows, i, i + dd)
    return rows, x17


def _select_threshold(v1, x1, v2, x2):
    n = v1[0].shape[1]
    sub = lax.broadcasted_iota(jnp.int32, (SUBLANE, n), 0)

    def pack(vals):
        out = vals[SUBLANE - 1]
        for b in range(SUBLANE - 2, -1, -1):
            out = jnp.where(sub == b, vals[b], out)
        return out

    v2lo, v2hi, v1hi = pack(v2[:SUBLANE]), pack(v2[SUBLANE:]), pack(v1[SUBLANE:])
    cands = ([v1[0] + v2lo, v1[0] + v2hi] + [v1[a] + v2lo for a in range(1, SUBLANE)]
             + [v1hi + v2[0]])
    c, c17 = _top17(cands + [None] * (PEER_TOPK - len(cands)))
    z = functools.reduce(jnp.add, [jnp.exp((ck - c[0])[0:1]) for ck in c])
    c17 = jnp.maximum(c17, jnp.maximum(x1 + v2[0], v1[0] + x2))
    return 0.5 * (c[PEER_TOPK - 1] + c17)[0:1], z


GELU_C0 = 0.7978845608028654
GELU_C1 = GELU_C0 * 0.044715


def _gelu_tanh_x2(x):
    return x * (1.0 + jnp.tanh(x * (GELU_C0 + GELU_C1 * (x * x))))


PEER_EB = 2048
PEER_SB = 512


def _peer_kernel(h2_ref, wqt_ref, keys_ref, u_ref, un_ref, vt_ref, vtp_ref, o_ref,
                 h2t_ref, s2_ref, thr_ref, aiw_ref, bj_ref, acc_ref, s_first_ref, coef_last_ref,
                 top_ref):
    e = pl.program_id(1)
    ne = pl.num_programs(1)
    tb = h2_ref.shape[0]

    @pl.when(e == 0)
    def _():
        h2t_ref[...] = h2_ref[...].astype(F32).T.astype(BF16)
        qt = _dot(wqt_ref[...], h2t_ref[...])
        for h in range(PEER_HEADS):
            q1 = qt[(2 * h) * D_HALF:(2 * h + 1) * D_HALF].astype(BF16)
            q2 = qt[(2 * h + 1) * D_HALF:(2 * h + 2) * D_HALF].astype(BF16)
            s1 = _dot(keys_ref[2 * h], q1)
            s2 = _dot(keys_ref[2 * h + 1], q2)
            for p, s in enumerate((s1, s2)):
                top, x17 = _top17([s[r * SUBLANE:(r + 1) * SUBLANE] for r in range(PEER_TOPK)])
                for r, row in enumerate(top + [x17]):
                    top_ref[p, r] = row
            v1 = [top_ref[0, r] for r in range(PEER_TOPK)]
            v2 = [top_ref[1, r] for r in range(PEER_TOPK)]
            tau, z = _select_threshold(v1, top_ref[0, PEER_TOPK], v2, top_ref[1, PEER_TOPK])
            s2_ref[h] = s2
            thr_ref[h] = tau - s1
            ai = jnp.exp(s1 - v1[0][0:1]).astype(BF16).astype(F32)
            bits = pltpu.bitcast(ai, jnp.uint32)
            aiw_ref[h] = bits | (bits >> 16)
            bj_ref[h] = (jnp.exp(s2 - v2[0][0:1]) * (0.5 / z)).astype(BF16)
        acc_ref[...] = jnp.zeros_like(acc_ref)
        coef_last_ref[...] = jnp.zeros_like(coef_last_ref)
        s_first_ref[...] = _dot(u_ref[0:PEER_SB, :], h2t_ref[...])

    nsb = PEER_EB // PEER_SB
    nib = PEER_SB // N_KEYS
    halves = [(0, tb)] if tb < 4 * LANE else [(0, tb // 2), (tb // 2, tb)]

    def pre_acts(k, lo, hi):
        rows = un_ref[...] if k == nsb else u_ref[k * PEER_SB:(k + 1) * PEER_SB, :]
        return _dot(rows, h2t_ref[:, lo:hi])

    def coefficients(k, ii, s):
        i = e * (PEER_EB // N_KEYS) + k * nib + ii
        act = _gelu_tanh_x2(s[ii * N_KEYS:(ii + 1) * N_KEYS]).astype(BF16)
        w = None
        for h in range(PEER_HEADS):
            hit = s2_ref[h] >= thr_ref[h, pl.ds(i, 1), :]
            ai = pltpu.bitcast(jnp.broadcast_to(aiw_ref[h, pl.ds(i, 1), :], (SUBLANE, tb)), BF16)
            ai = jnp.concatenate([ai] * (N_KEYS // (2 * SUBLANE)), axis=0)
            wh = jnp.where(hit, ai * bj_ref[h], jnp.zeros((), BF16))
            w = wh if w is None else w + wh
        return w * act

    def accumulate(vt_sub, coef, lo, hi):
        acc_ref[:, lo:hi] += _dot(vt_sub, coef[:, lo:hi])

    s_cur = s_first_ref[...]
    coef_prev = coef_last_ref[...]
    for k in range(nsb):
        vt_prev = vtp_ref[...] if k == 0 else vt_ref[:, (k - 1) * PEER_SB:k * PEER_SB]
        tasks = [(kind, lo, hi) for lo, hi in halves for kind in ("pre", "acc")]
        s_parts, coef = [], []
        for ii in range(nib):
            coef.append(coefficients(k, ii, s_cur))
            for kind, lo, hi in tasks[ii * len(tasks) // nib:(ii + 1) * len(tasks) // nib]:
                if kind == "pre":
                    s_parts.append(pre_acts(k + 1, lo, hi))
                else:
                    accumulate(vt_prev, coef_prev, lo, hi)
        s_cur = jnp.concatenate(s_parts, axis=1)
        coef_prev = jnp.concatenate(coef, axis=0)
    s_first_ref[...] = s_cur
    coef_last_ref[...] = coef_prev

    @pl.when(e == ne - 1)
    def _():
        accumulate(vt_ref[:, (nsb - 1) * PEER_SB:], coef_last_ref[...], 0, tb)
        o_ref[...] = acc_ref[...].T


def _peer(h2, wqt, keys, u, vt):
    n, d = h2.shape
    tb = min(n, 512)
    hs = (PEER_HEADS, N_KEYS, tb)
    ne, nsb = N_EXPERTS // PEER_EB, PEER_EB // PEER_SB
    return pl.pallas_call(
        _peer_kernel,
        grid=(n // tb, ne),
        in_specs=[pl.BlockSpec((tb, d), lambda i, e: (i, 0)),
                  pl.BlockSpec(wqt.shape, lambda i, e: (0, 0)),
                  pl.BlockSpec(keys.shape, lambda i, e: (0, 0, 0)),
                  pl.BlockSpec((PEER_EB, d), lambda i, e: (e, 0)),
                  pl.BlockSpec((PEER_SB, d), lambda i, e: (jnp.minimum(e + 1, ne - 1) * nsb, 0)),
                  pl.BlockSpec((d, PEER_EB), lambda i, e: (0, e)),
                  pl.BlockSpec((d, PEER_SB), lambda i, e: (0, jnp.maximum(e * nsb - 1, 0)))],
        out_specs=pl.BlockSpec((tb, d), lambda i, e: (i, 0)),
        out_shape=jax.ShapeDtypeStruct((n, d), F32),
        scratch_shapes=[pltpu.VMEM((d, tb), BF16), pltpu.VMEM(hs, F32), pltpu.VMEM(hs, F32),
                        pltpu.VMEM(hs, jnp.uint32), pltpu.VMEM(hs, BF16), pltpu.VMEM((d, tb), F32),
                        pltpu.VMEM((PEER_SB, tb), F32), pltpu.VMEM((PEER_SB, tb), BF16),
                        pltpu.VMEM((2, PEER_TOPK + 1, SUBLANE, tb), F32)],
        compiler_params=_cparams(("arbitrary", "arbitrary")),
        name="peer",
    )(h2, wqt, keys, u, u, vt, vt)


def _final_kernel(x2_ref, p_ref, mod_ref, g_ref, o_ref, *, normalize):
    x = x2_ref[...] + mod_ref[...][5:6] * p_ref[...]
    if normalize:
        x = x * lax.rsqrt(jnp.mean(x * x, axis=-1, keepdims=True) + EPS) * g_ref[...]
    o_ref[...] = x


def _final(x2, peer_out, mod, g, normalize):
    b, t, d = x2.shape
    tm = min(t, 1024)
    spec = pl.BlockSpec((None, tm, d), lambda bi, i: (bi, i, 0))
    return pl.pallas_call(
        functools.partial(_final_kernel, normalize=normalize),
        grid=(b, t // tm),
        in_specs=[spec, spec,
                  pl.BlockSpec((None, 6, d), lambda bi, i: (bi, 0, 0)),
                  pl.BlockSpec((1, d), lambda bi, i: (0, 0))],
        out_specs=spec,
        out_shape=jax.ShapeDtypeStruct((b, t, d), F32),
        compiler_params=_cparams(("arbitrary", "arbitrary")),
        name="final_norm",
    )(x2, peer_out, mod, g.reshape(1, d))


def _prep_layer_weights(w_in, conv_w, conv_b, dt_bias, a_log, d_skip, ssm_norm_g, w_branch_ssm,
                        w_branch_attn, w_out, peer_wq, peer_keys, peer_u, peer_v):
    d = w_in.shape[0]
    off_xbc = D_INNER
    off_dt = off_xbc + CONV_DIM
    off_q = off_dt + SSM_HEADS
    off_k = off_q + ATTN_DIM
    off_v = off_k + KV_DIM
    off_gate = off_v + KV_DIM
    w_re = jnp.concatenate([
        w_in[:, off_xbc:off_dt], w_in[:, :off_xbc], w_in[:, off_gate:], w_in[:, off_q:off_k],
        w_in[:, off_k:off_v], w_in[:, off_v:off_gate]], axis=1).astype(BF16)
    w_dt = jnp.pad(w_in[:, off_dt:off_q], ((0, 0), (0, DT_PAD - SSM_HEADS))).astype(BF16)
    head_of_lane = jnp.arange(D_INNER) // SSM_HEADDIM
    e1 = (jnp.arange(SSM_HEADS)[:, None] == head_of_lane[None, :]).astype(BF16)
    return dict(
        w_re=w_re, w_dt=w_dt, conv_w=conv_w, conv_b=conv_b.reshape(1, CONV_DIM),
        dtb=dt_bias.reshape(1, SSM_HEADS), alog=a_log.reshape(1, SSM_HEADS),
        dsk_x=jnp.repeat(d_skip, SSM_HEADDIM).reshape(1, D_INNER),
        norm_g=ssm_norm_g.reshape(1, D_INNER), e3=jnp.concatenate([e1, e1, e1], axis=0),
        wbs=w_branch_ssm.astype(BF16), wba=w_branch_attn.astype(BF16), wo=w_out.astype(BF16),
        wqt=peer_wq.T.astype(BF16),
        keys=peer_keys.reshape(PEER_HEADS * 2, N_KEYS, D_HALF).astype(BF16),
        u=peer_u.astype(BF16), vt=peer_v.T.astype(BF16))


def _layer(x, mod, conv_hist, ssm0, k_hist, v_hist, norm1_g, norm2_g, sinks, lw):
    b, t, d = x.shape
    proj, dt_raw = _proj(x, mod, norm1_g, lw["w_re"], lw["w_dt"])
    hist8 = jnp.pad(conv_hist, ((0, 0), (SUBLANE - (CONV_W - 1), 0), (0, 0)))
    h0 = ssm0.reshape(b, SSM_GROUPS, GROUP_W, D_STATE)
    y_ssm, h_new = _ssd(proj, dt_raw, hist8, h0, lw["conv_w"], lw["conv_b"], lw["dtb"], lw["alog"],
                        lw["dsk_x"], lw["norm_g"], lw["e3"])
    if k_hist is not None:
        k_hist = k_hist.reshape(b, -1, KV_DIM)
        v_hist = v_hist.reshape(b, -1, KV_DIM)
    o_attn = _attn(proj, k_hist, v_hist, sinks)
    x2, h2 = _merge(x, y_ssm, o_attn, proj, mod, norm2_g, lw["wbs"], lw["wba"], lw["wo"])
    peer_out = _peer(h2.reshape(b * t, d), lw["wqt"], lw["keys"], lw["u"], lw["vt"]).reshape(b, t, d)
    keep = WINDOW if k_hist is None else t
    k_new = proj[:, t - keep:, COL_K:COL_K + KV_DIM].astype(F32).reshape(b, keep, N_KV, HEAD_DIM)
    v_new = proj[:, t - keep:, COL_V:COL_V + KV_DIM].astype(F32).reshape(b, keep, N_KV, HEAD_DIM)
    conv_new = proj[:, t - (CONV_W - 1):, COL_XBC:COL_XBC + CONV_DIM].astype(F32)
    ssm_new = h_new.reshape(b, SSM_HEADS, SSM_HEADDIM, D_STATE)
    return x2, peer_out, k_new, v_new, conv_new, ssm_new


def kernel(x_prompt, x_sample, cache_attn_k, cache_attn_v, state_conv, state_ssm, c_prompt, c_sample, ada_w, ada_b, norm1_g, w_in, conv_w, conv_b, dt_bias, a_log, d_skip, ssm_norm_g, attn_sinks, w_branch_ssm, w_branch_attn, w_out, norm2_g, peer_wq, peer_keys, peer_u, peer_v, final_g):
    depth = ada_w.shape[0]
    d = x_prompt.shape[-1]
    bp, bs = x_prompt.shape[0], x_sample.shape[0]
    xp, xs = x_prompt, x_sample
    c_all = jnp.concatenate([c_prompt, c_sample], axis=0)
    outs = [[] for _ in range(8)]
    for l in range(depth):
        lw = _prep_layer_weights(w_in[l], conv_w[l], conv_b[l], dt_bias[l], a_log[l], d_skip[l],
                                 ssm_norm_g[l], w_branch_ssm[l], w_branch_attn[l], w_out[l],
                                 peer_wq[l], peer_keys[l], peer_u[l], peer_v[l])
        mod = _adaln(c_all, ada_w[l], ada_b[l]).reshape(bp + bs, 6, d)
        last = l == depth - 1
        conv0 = jnp.zeros((bp, CONV_W - 1, CONV_DIM), F32)
        ssm0 = jnp.zeros((bp, SSM_HEADS, SSM_HEADDIM, D_STATE), F32)
        res = []
        for x, m, ch, s0, kh, vh in ((xp, mod[:bp], conv0, ssm0, None, None),
                                     (xs, mod[bp:], state_conv[l], state_ssm[l],
                                      cache_attn_k[l], cache_attn_v[l])):
            x2, peer_out, kn, vn, cn, sn = _layer(x, m, ch, s0, kh, vh, norm1_g[l], norm2_g[l],
                                                  attn_sinks[l], lw)
            res.append((_final(x2, peer_out, m, final_g, last), kn, vn, cn, sn))
        (xp, kp, vp, cp, sp), (xs, kn, vn, cn, sn) = res
        for lst, val in zip(outs, (kp, vp, cp, sp, kn, vn, cn, sn)):
            lst.append(val)
    return (xp, xs) + tuple(jnp.stack(o) for o in outs)
```

```python
import functools

import jax
import jax.numpy as jnp
from jax import lax
from jax.experimental import pallas as pl
from jax.experimental.pallas import tpu as pltpu

F32 = jnp.float32
BF16 = jnp.bfloat16

EPS = 1e-6
CHUNK = 64
WINDOW = 128
SSM_HEADS = 32
SSM_HEADDIM = 64
D_STATE = 128
SSM_GROUPS = 8
HEADS_PER_GROUP = SSM_HEADS // SSM_GROUPS
GROUP_W = HEADS_PER_GROUP * SSM_HEADDIM
D_INNER = SSM_HEADS * SSM_HEADDIM
CONV_W = 4
CONV_DIM = D_INNER + 2 * SSM_GROUPS * D_STATE
N_HEADS = 16
N_KV = 4
HEAD_DIM = 64
Q_PER_KV = N_HEADS // N_KV
ATTN_DIM = N_HEADS * HEAD_DIM
KV_DIM = N_KV * HEAD_DIM
ATTN_SCALE = HEAD_DIM ** -0.5
PEER_HEADS = 8
N_KEYS = 128
N_EXPERTS = N_KEYS * N_KEYS
PEER_TOPK = 16
D_HALF = 128

LANE = 128
SUBLANE = 8
DT_PAD = LANE

COL_XBC = 0
COL_Z = COL_XBC + CONV_DIM
COL_GATE = COL_Z + D_INNER
COL_Q = COL_GATE + 2048
COL_K = COL_Q + ATTN_DIM
COL_V = COL_K + KV_DIM
PROJ_W = COL_V + KV_DIM
PROJ_TN = 2432
CONV_CB = 1024

VMEM_LIMIT = 56 * 1024 * 1024


def _cparams(sem):
    return pltpu.CompilerParams(dimension_semantics=sem, vmem_limit_bytes=VMEM_LIMIT)


def _sigmoid(x):
    return 1.0 / (1.0 + jnp.exp(-x))


def _split2(v):
    hi = v.astype(BF16)
    lo = (v - hi.astype(F32)).astype(BF16)
    return hi, lo


def _split3(v):
    hi = v.astype(BF16)
    r = v - hi.astype(F32)
    mid = r.astype(BF16)
    lo = (r - mid.astype(F32)).astype(BF16)
    return hi, mid, lo


def _dot(a, b):
    return jnp.dot(a, b, preferred_element_type=F32)


def _dot_nt(a, b):
    return lax.dot_general(a, b, (((1,), (1,)), ((), ())), preferred_element_type=F32)


def _dot_tn(a, b):
    return lax.dot_general(a, b, (((0,), (0,)), ((), ())), preferred_element_type=F32)


def _adaln_kernel(c_ref, w_ref, b_ref, o_ref):
    c = c_ref[...]
    s = c * _sigmoid(c)
    s_hi, s_lo = _split2(s)
    w_hi, w_lo = _split2(w_ref[...])
    acc = _dot(s_hi, w_hi) + _dot(s_hi, w_lo) + _dot(s_lo, w_hi)
    o_ref[...] = acc + b_ref[...]


def _adaln(c, ada_w, ada_b):
    rows, d = c.shape
    n = ada_w.shape[1]
    tn = 1024
    return pl.pallas_call(
        _adaln_kernel,
        grid=(n // tn,),
        in_specs=[pl.BlockSpec((rows, d), lambda j: (0, 0)),
                  pl.BlockSpec((d, tn), lambda j: (0, j)),
                  pl.BlockSpec((1, tn), lambda j: (0, j))],
        out_specs=pl.BlockSpec((rows, tn), lambda j: (0, j)),
        out_shape=jax.ShapeDtypeStruct((rows, n), F32),
        compiler_params=_cparams(("arbitrary",)),
        name="adaln",
    )(c, ada_w, ada_b.reshape(1, n))


def _proj_kernel(x_ref, mod_ref, g_ref, w_ref, wdt_ref, o_ref, dt_ref, h_ref):
    @pl.when(pl.program_id(2) == 0)
    def _():
        x = x_ref[...]
        ms = jnp.mean(x * x, axis=-1, keepdims=True)
        xn = x * lax.rsqrt(ms + EPS) * g_ref[...]
        m = mod_ref[...]
        h_ref[...] = (xn * (1.0 + m[1:2]) + m[0:1]).astype(BF16)
        dt_ref[...] = _dot(h_ref[...], wdt_ref[...])

    o_ref[...] = _dot(h_ref[...], w_ref[...]).astype(o_ref.dtype)


def _proj(x, mod, g, w, w_dt):
    b, t, d = x.shape
    tm = min(t, 1024)
    return pl.pallas_call(
        _proj_kernel,
        grid=(b, t // tm, PROJ_W // PROJ_TN),
        in_specs=[pl.BlockSpec((None, tm, d), lambda bi, i, j: (bi, i, 0)),
                  pl.BlockSpec((None, 6, d), lambda bi, i, j: (bi, 0, 0)),
                  pl.BlockSpec((1, d), lambda bi, i, j: (0, 0)),
                  pl.BlockSpec((d, PROJ_TN), lambda bi, i, j: (0, j)),
                  pl.BlockSpec((d, DT_PAD), lambda bi, i, j: (0, 0))],
        out_specs=[pl.BlockSpec((None, tm, PROJ_TN), lambda bi, i, j: (bi, i, j)),
                   pl.BlockSpec((None, tm, DT_PAD), lambda bi, i, j: (bi, i, 0))],
        out_shape=[jax.ShapeDtypeStruct((b, t, PROJ_W), BF16),
                   jax.ShapeDtypeStruct((b, t, DT_PAD), F32)],
        scratch_shapes=[pltpu.VMEM((tm, d), BF16)],
        compiler_params=_cparams(("arbitrary", "arbitrary", "arbitrary")),
        name="norm_proj",
    )(x, mod, g.reshape(1, d), w, w_dt)


def _softplus(x):
    return jnp.maximum(x, 0.0) + jnp.log1p(jnp.exp(-jnp.abs(x)))


def _ssd_kernel(xbc_ref, z_ref, dt_ref, hist_ref, h0_ref, cw_ref, cb_ref, dtb_ref, alog_ref,
                dsk_ref, ng_ref, e3_ref, y_ref, hout_ref, hcat_ref, tail_ref, xc_ref, *, L):
    c = pl.program_id(1)
    nc = pl.num_programs(1)

    @pl.when(c == 0)
    def _():
        tail_ref[...] = hist_ref[...]
        for g in range(SSM_GROUPS):
            hcat_ref[g] = h0_ref[g].T

    row = lax.broadcasted_iota(jnp.int32, (L, L), 0)
    col = lax.broadcasted_iota(jnp.int32, (L, L), 1)
    causal = row >= col

    shifts = jnp.concatenate([jnp.where(row - col == k, 1.0, 0.0) for k in range(1, CONV_W)],
                             axis=0).astype(BF16)
    hp = tail_ref[...]
    row8 = lax.broadcasted_iota(jnp.int32, (SUBLANE, CONV_CB), 0)
    for cl in range(0, CONV_DIM, CONV_CB):
        ch = cl + CONV_CB
        xb = xbc_ref[:, cl:ch]
        x = xb.astype(F32)
        sh = _dot(shifts, xb)
        acc = cb_ref[:, cl:ch] + x * cw_ref[CONV_W - 1:CONV_W, cl:ch]
        for k in range(1, CONV_W):
            xr = sh[(k - 1) * L:k * L]
            top = jnp.where(row8 < k, pltpu.roll(hp[:, cl:ch], k, axis=0), xr[:SUBLANE])
            xr = jnp.concatenate([top, xr[SUBLANE:]], axis=0)
            acc = acc + xr * cw_ref[CONV_W - 1 - k:CONV_W - k, cl:ch]
        tail_ref[:, cl:ch] = x[L - SUBLANE:]
        xc_ref[:, cl:ch] = acc * _sigmoid(acc)

    dt = _softplus(dt_ref[...][:, :SSM_HEADS] + dtb_ref[...])
    a = dt * (-jnp.exp(alog_ref[...]))
    tril = jnp.where(causal, 1.0, 0.0).astype(BF16)
    cs = _dot(tril, jnp.concatenate(_split3(a), axis=1))
    acum = cs[:, :SSM_HEADS] + cs[:, SSM_HEADS:2 * SSM_HEADS] + cs[:, 2 * SSM_HEADS:]
    sel_r = lax.broadcasted_iota(jnp.int32, (SSM_HEADS, 3 * SSM_HEADS), 0)
    sel_c = lax.broadcasted_iota(jnp.int32, (SSM_HEADS, 3 * SSM_HEADS), 1)
    sel = jnp.where(sel_c % SSM_HEADS == sel_r, 1.0, 0.0).astype(BF16)
    acum_t = _dot_nt(sel, jnp.concatenate(_split3(acum), axis=1))
    a_last = acum[L - 1:L]
    fac = jnp.concatenate([dt, jnp.exp(acum), jnp.exp(a_last - acum)], axis=0)
    fac3 = jnp.concatenate(_split3(fac), axis=1)
    lane_head = lax.broadcasted_iota(jnp.int32, (L, GROUP_W), 1) // SSM_HEADDIM

    for g in range(SSM_GROUPS):
        lo, hi = g * GROUP_W, (g + 1) * GROUP_W
        xs_g = xc_ref[:, lo:hi]
        bm_g = xc_ref[:, D_INNER + g * D_STATE:D_INNER + (g + 1) * D_STATE].astype(BF16)
        cm_g = xc_ref[:, D_INNER + SSM_GROUPS * D_STATE + g * D_STATE:
                      D_INNER + SSM_GROUPS * D_STATE + (g + 1) * D_STATE].astype(BF16)
        fx = _dot(fac3, e3_ref[:, lo:hi])
        dt_x, ea_x, te_x = fx[:L], fx[L:2 * L], fx[2 * L:]
        xdt = xs_g * dt_x
        xdt_b = xdt.astype(BF16)
        cb = _dot_nt(cm_g, bm_g)
        ms = []
        for r in range(HEADS_PER_GROUP):
            h = g * HEADS_PER_GROUP + r
            seg = acum[:, h:h + 1] - acum_t[h:h + 1, :]
            dec = jnp.exp(jnp.where(causal, seg, -jnp.inf))
            ms.append((cb * dec).astype(BF16))
        yy = _dot(jnp.concatenate(ms, axis=0), xdt_b)
        y = yy[3 * L:]
        for r in range(HEADS_PER_GROUP - 2, -1, -1):
            y = jnp.where(lane_head == r, yy[r * L:(r + 1) * L], y)
        hc = hcat_ref[g]
        y = y + _dot(cm_g, hc.astype(BF16)) * ea_x
        hcat_ref[g] = ea_x[L - 1:L] * hc + _dot_tn(bm_g, (te_x * xdt).astype(BF16))
        y = y + dsk_ref[:, lo:hi] * xs_g
        zg = z_ref[:, lo:hi].astype(F32)
        y = y * (zg * _sigmoid(zg))
        y = y * lax.rsqrt(jnp.mean(y * y, axis=-1, keepdims=True) + EPS) * ng_ref[:, lo:hi]
        y_ref[:, lo:hi] = y.astype(y_ref.dtype)

    @pl.when(c == nc - 1)
    def _():
        for g in range(SSM_GROUPS):
            hout_ref[g] = hcat_ref[g].T


def _ssd(proj, dt_raw, hist8, h0, conv_w, conv_b, dtb, alog, dsk_x, norm_g, e3):
    b, t, _ = proj.shape
    L = min(t, 256)
    kern = functools.partial(_ssd_kernel, L=L)
    full = lambda shape: pl.BlockSpec(shape, lambda bi, ci: (0,) * len(shape))
    return pl.pallas_call(
        kern,
        grid=(b, t // L),
        in_specs=[pl.BlockSpec((None, L, CONV_DIM), lambda bi, ci: (bi, ci, COL_XBC // CONV_DIM)),
                  pl.BlockSpec((None, L, D_INNER), lambda bi, ci: (bi, ci, COL_Z // D_INNER)),
                  pl.BlockSpec((None, L, DT_PAD), lambda bi, ci: (bi, ci, 0)),
                  pl.BlockSpec((None, SUBLANE, CONV_DIM), lambda bi, ci: (bi, 0, 0)),
                  pl.BlockSpec((None, SSM_GROUPS, GROUP_W, D_STATE), lambda bi, ci: (bi, 0, 0, 0)),
                  full((CONV_W, CONV_DIM)), full((1, CONV_DIM)), full((1, SSM_HEADS)),
                  full((1, SSM_HEADS)), full((1, D_INNER)), full((1, D_INNER)),
                  full((3 * SSM_HEADS, D_INNER))],
        out_specs=[pl.BlockSpec((None, L, D_INNER), lambda bi, ci: (bi, ci, 0)),
                   pl.BlockSpec((None, SSM_GROUPS, GROUP_W, D_STATE), lambda bi, ci: (bi, 0, 0, 0))],
        out_shape=[jax.ShapeDtypeStruct((b, t, D_INNER), BF16),
                   jax.ShapeDtypeStruct((b, SSM_GROUPS, GROUP_W, D_STATE), F32)],
        scratch_shapes=[pltpu.VMEM((SSM_GROUPS, D_STATE, GROUP_W), F32),
                        pltpu.VMEM((SUBLANE, CONV_DIM), F32), pltpu.VMEM((L, CONV_DIM), F32)],
        compiler_params=_cparams(("arbitrary", "arbitrary")),
        name="ssd",
    )(proj, proj, dt_raw, hist8, h0, conv_w, conv_b, dtb, alog, dsk_x, norm_g, e3)


def _attn_kernel(sink_ref, q_ref, kp_ref, vp_ref, kc_ref, vc_ref, o_ref, *, tq, npast, banded):
    t = pl.program_id(1)
    cq = CHUNK if banded else tq
    nkw = npast + cq
    q = (q_ref[...] * ATTN_SCALE).astype(BF16)
    k = jnp.concatenate([kp_ref[...].astype(BF16), kc_ref[...].astype(BF16)], axis=0)
    v = jnp.concatenate([vp_ref[...].astype(BF16), vc_ref[...].astype(BF16)], axis=0)
    blocks = [(c, g) for c in range(tq // cq) for g in range(N_KV)]
    kj = lax.broadcasted_iota(jnp.int32, (1, nkw), 1)

    scores = []
    for c, g in blocks:
        qs = jnp.concatenate([q[c * cq:(c + 1) * cq, (g * Q_PER_KV + r) * HEAD_DIM:
                                (g * Q_PER_KV + r + 1) * HEAD_DIM] for r in range(Q_PER_KV)], axis=0)
        s = _dot_nt(qs, k[c * cq:c * cq + nkw, g * HEAD_DIM:(g + 1) * HEAD_DIM])
        if banded:
            s = s + jnp.where(t * tq + c * cq - npast + kj >= 0, 0.0, -jnp.inf)
        scores.append(s)
    sinks = [jnp.concatenate([jnp.full((cq, 1), sink_ref[g * Q_PER_KV + r], F32)
                              for r in range(Q_PER_KV)], axis=0) for g in range(N_KV)]
    ms = [jnp.maximum(jnp.max(s, axis=-1, keepdims=True), sinks[g]) for (c, g), s in zip(blocks, scores)]
    es = [jnp.exp(s - m) for s, m in zip(scores, ms)]
    dens = [jnp.sum(e, axis=-1, keepdims=True) for e in es]
    invs = [1.0 / (den + jnp.exp(sinks[g] - m)) for (c, g), den, m in zip(blocks, dens, ms)]
    outs = [_dot(e.astype(BF16), v[c * cq:c * cq + nkw, g * HEAD_DIM:(g + 1) * HEAD_DIM])
            for (c, g), e in zip(blocks, es)]
    outs = [o * inv for o, inv in zip(outs, invs)]
    for (c, g), o in zip(blocks, outs):
        for r in range(Q_PER_KV):
            h = g * Q_PER_KV + r
            o_ref[c * cq:(c + 1) * cq, h * HEAD_DIM:(h + 1) * HEAD_DIM] = (
                o[r * cq:(r + 1) * cq].astype(o_ref.dtype))


def _attn(proj, k_hist, v_hist, sinks):
    b, t, _ = proj.shape
    banded = k_hist is None
    tq = min(t, 256)
    npast = WINDOW
    kw, kb, vb = KV_DIM, COL_K // KV_DIM, COL_V // KV_DIM
    if banded:
        blocks_per_tile = tq // npast
        prev = lambda col: pl.BlockSpec(
            (None, npast, kw), lambda bi, ti: (bi, jnp.maximum(ti * blocks_per_tile - 1, 0), col))
        kp_spec, vp_spec, kp, vp = prev(kb), prev(vb), proj, proj
    else:
        kp_spec = vp_spec = pl.BlockSpec((None, npast, kw), lambda bi, ti: (bi, 0, 0))
        kp, vp = k_hist, v_hist
    kern = functools.partial(_attn_kernel, tq=tq, npast=npast, banded=banded)
    return pl.pallas_call(
        kern,
        grid=(b, t // tq),
        in_specs=[pl.BlockSpec(memory_space=pltpu.SMEM),
                  pl.BlockSpec((None, tq, ATTN_DIM), lambda bi, ti: (bi, ti, COL_Q // ATTN_DIM)),
                  kp_spec, vp_spec,
                  pl.BlockSpec((None, tq, kw), lambda bi, ti: (bi, ti, kb)),
                  pl.BlockSpec((None, tq, kw), lambda bi, ti: (bi, ti, vb))],
        out_specs=pl.BlockSpec((None, tq, ATTN_DIM), lambda bi, ti: (bi, ti, 0)),
        out_shape=jax.ShapeDtypeStruct((b, t, ATTN_DIM), BF16),
        compiler_params=_cparams(("arbitrary", "arbitrary")),
        name="attn",
    )(sinks, proj, kp, vp, proj, proj)


def _merge_kernel(x_ref, y_ref, o_ref, gate_ref, mod_ref, g2_ref, wbs_ref, wba_ref, wo_ref,
                  x2_ref, h2_ref):
    d = x_ref.shape[-1]
    a = _dot(y_ref[...], wbs_ref[...])
    b = _dot(o_ref[...], wba_ref[...])
    gt = gate_ref[...].astype(F32)
    merged = _sigmoid(gt[:, :d]) * a + _sigmoid(gt[:, d:]) * b
    m = mod_ref[...]
    x2 = x_ref[...] + m[2:3] * _dot(merged.astype(BF16), wo_ref[...])
    x2_ref[...] = x2
    xn = x2 * lax.rsqrt(jnp.mean(x2 * x2, axis=-1, keepdims=True) + EPS) * g2_ref[...]
    h2_ref[...] = (xn * (1.0 + m[4:5]) + m[3:4]).astype(h2_ref.dtype)


def _merge(x, y_ssm, o_attn, proj, mod, g2, wbs, wba, wo):
    b, t, d = x.shape
    tm = min(t, 512)
    full = lambda shape: pl.BlockSpec(shape, lambda bi, i: (0,) * len(shape))
    return pl.pallas_call(
        _merge_kernel,
        grid=(b, t // tm),
        in_specs=[pl.BlockSpec((None, tm, d), lambda bi, i: (bi, i, 0)),
                  pl.BlockSpec((None, tm, D_INNER), lambda bi, i: (bi, i, 0)),
                  pl.BlockSpec((None, tm, ATTN_DIM), lambda bi, i: (bi, i, 0)),
                  pl.BlockSpec((None, tm, 2 * d), lambda bi, i: (bi, i, COL_GATE // (2 * d))),
                  pl.BlockSpec((None, 6, d), lambda bi, i: (bi, 0, 0)),
                  full((1, d)), full((D_INNER, d)), full((ATTN_DIM, d)), full((d, d))],
        out_specs=[pl.BlockSpec((None, tm, d), lambda bi, i: (bi, i, 0)),
                   pl.BlockSpec((None, tm, d), lambda bi, i: (bi, i, 0))],
        out_shape=[jax.ShapeDtypeStruct((b, t, d), F32), jax.ShapeDtypeStruct((b, t, d), BF16)],
        compiler_params=_cparams(("arbitrary", "arbitrary")),
        name="merge",
    )(x, y_ssm, o_attn, proj, mod, g2.reshape(1, d), wbs, wba, wo)


def _batcher_pairs(n):
    pairs = []
    p = 1
    while p < n:
        k = p
        while k >= 1:
            for j in range(k % p, n - k, 2 * k):
                for i in range(min(k, n - j - k)):
                    if (i + j) // (2 * p) == (i + j + k) // (2 * p):
                        pairs.append((i + j, i + j + k))
            k //= 2
        p *= 2
    return tuple(pairs)


_SORT16 = _batcher_pairs(PEER_TOPK)


def _max2(a, b):
    return b if a is None else a if b is None else jnp.maximum(a, b)


def _min2(a, b):
    return None if a is None or b is None else jnp.minimum(a, b)


def _cmpx(rows, i, j):
    rows[i], rows[j] = _max2(rows[i], rows[j]), _min2(rows[i], rows[j])


def _top17(rows):
    rows = list(rows)
    for i, j in _SORT16:
        _cmpx(rows, i, j)
    x17 = None
    for sh in (4, 2, 1):
        roll = lambda a: None if a is None else pltpu.roll(a, sh, axis=0)
        other = [roll(rows[PEER_TOPK - 1 - r]) for r in range(PEER_TOPK)]
        lo = [_min2(rows[r], other[r]) for r in range(PEER_TOPK)]
        rows = [_max2(rows[r], other[r]) for r in range(PEER_TOPK)]
        x17 = functools.reduce(_max2, lo + [x17, roll(x17)])
        for dd in (8, 4, 2, 1):
            for i in range(PEER_TOPK):
                if i & dd == 0:
                    _cmpx(rows, i, i + dd)
    return rows, x17


def _select_threshold(v1, x1, v2, x2):
    n = v1[0].shape[1]
    sub = lax.broadcasted_iota(jnp.int32, (SUBLANE, n), 0)

    def pack(vals):
        out = vals[SUBLANE - 1]
        for b in range(SUBLANE - 2, -1, -1):
            out = jnp.where(sub == b, vals[b], out)
        return out

    v2lo, v2hi, v1hi = pack(v2[:SUBLANE]), pack(v2[SUBLANE:]), pack(v1[SUBLANE:])
    cands = ([v1[0] + v2lo, v1[0] + v2hi] + [v1[a] + v2lo for a in range(1, SUBLANE)]
             + [v1hi + v2[0]])
    c, c17 = _top17(cands + [None] * (PEER_TOPK - len(cands)))
    z = functools.reduce(jnp.add, [jnp.exp((ck - c[0])[0:1]) for ck in c])
    c17 = jnp.maximum(c17, jnp.maximum(x1 + v2[0], v1[0] + x2))
    return 0.5 * (c[PEER_TOPK - 1] + c17)[0:1], z


GELU_C0 = 0.7978845608028654
GELU_C1 = GELU_C0 * 0.044715


def _gelu_tanh_x2(x):
    return x * (1.0 + jnp.tanh(x * (GELU_C0 + GELU_C1 * (x * x))))


def _bf16_pair_words(x):
    bits = pltpu.bitcast(x.astype(BF16).astype(F32), jnp.uint32)
    return bits | (bits >> 16)


PEER_EB = 2048
PEER_SB = 512


def _peer_kernel(h2_ref, wqt_ref, keys_ref, u_ref, un_ref, vt_ref, vtp_ref, o_ref,
                 h2t_ref, rank_ref, cntw_ref, aiw_ref, bj_ref, acc_ref, s_first_ref, coef_last_ref,
                 top_ref):
    e = pl.program_id(1)
    ne = pl.num_programs(1)
    tb = h2_ref.shape[0]

    @pl.when(e == 0)
    def _():
        h2t_ref[...] = h2_ref[...].astype(F32).T.astype(BF16)
        qt = _dot(wqt_ref[...], h2t_ref[...])
        for h in range(PEER_HEADS):
            q1 = qt[(2 * h) * D_HALF:(2 * h + 1) * D_HALF].astype(BF16)
            q2 = qt[(2 * h + 1) * D_HALF:(2 * h + 2) * D_HALF].astype(BF16)
            s1 = _dot(keys_ref[2 * h], q1)
            s2 = _dot(keys_ref[2 * h + 1], q2)
            for p, s in enumerate((s1, s2)):
                top, x17 = _top17([s[r * SUBLANE:(r + 1) * SUBLANE] for r in range(PEER_TOPK)])
                for r, row in enumerate(top + [x17]):
                    top_ref[p, r] = row
            v1 = [top_ref[0, r] for r in range(PEER_TOPK)]
            v2 = [top_ref[1, r] for r in range(PEER_TOPK)]
            tau, z = _select_threshold(v1, top_ref[0, PEER_TOPK], v2, top_ref[1, PEER_TOPK])
            thr = tau - s1
            rank = jnp.zeros_like(s2)
            cnt = jnp.zeros_like(s1)
            for r in range(PEER_TOPK):
                v2r = jnp.concatenate([v2[r]] * (N_KEYS // SUBLANE), axis=0)
                rank = jnp.where(v2r > s2, r + 1.0, rank)
                cnt = jnp.where(v2r >= thr, r + 1.0, cnt)
            rank_ref[h] = rank.astype(BF16)
            cntw_ref[h] = _bf16_pair_words(cnt)
            aiw_ref[h] = _bf16_pair_words(jnp.exp(s1 - v1[0][0:1]))
            bj_ref[h] = (jnp.exp(s2 - v2[0][0:1]) * (0.5 / z)).astype(BF16)
        acc_ref[...] = jnp.zeros_like(acc_ref)
        coef_last_ref[...] = jnp.zeros_like(coef_last_ref)
        s_first_ref[...] = _dot(u_ref[0:PEER_SB, :], h2t_ref[...])

    nsb = PEER_EB // PEER_SB
    nib = PEER_SB // N_KEYS
    halves = [(0, tb)] if tb < 4 * LANE else [(0, tb // 2), (tb // 2, tb)]

    def pre_acts(k, lo, hi):
        rows = un_ref[...] if k == nsb else u_ref[k * PEER_SB:(k + 1) * PEER_SB, :]
        return _dot(rows, h2t_ref[:, lo:hi])

    def coefficients(k, ii, s):
        i = e * (PEER_EB // N_KEYS) + k * nib + ii
        act = _gelu_tanh_x2(s[ii * N_KEYS:(ii + 1) * N_KEYS]).astype(BF16)
        w = None
        def row(ref, h):
            tile = pltpu.bitcast(jnp.broadcast_to(ref[h, pl.ds(i, 1), :], (SUBLANE, tb)), BF16)
            return jnp.concatenate([tile] * (N_KEYS // (2 * SUBLANE)), axis=0)

        for h in range(PEER_HEADS):
            hit = rank_ref[h] < row(cntw_ref, h)
            wh = jnp.where(hit, row(aiw_ref, h) * bj_ref[h], jnp.zeros((), BF16))
            w = wh if w is None else w + wh
        return w * act

    def accumulate(vt_sub, coef, lo, hi):
        acc_ref[:, lo:hi] += _dot(vt_sub, coef[:, lo:hi])

    s_cur = s_first_ref[...]
    coef_prev = coef_last_ref[...]
    for k in range(nsb):
        vt_prev = vtp_ref[...] if k == 0 else vt_ref[:, (k - 1) * PEER_SB:k * PEER_SB]
        tasks = [(kind, lo, hi) for lo, hi in halves for kind in ("pre", "acc")]
        s_parts, coef = [], []
        for ii in range(nib):
            coef.append(coefficients(k, ii, s_cur))
            for kind, lo, hi in tasks[ii * len(tasks) // nib:(ii + 1) * len(tasks) // nib]:
                if kind == "pre":
                    s_parts.append(pre_acts(k + 1, lo, hi))
                else:
                    accumulate(vt_prev, coef_prev, lo, hi)
        s_cur = jnp.concatenate(s_parts, axis=1)
        coef_prev = jnp.concatenate(coef, axis=0)
    s_first_ref[...] = s_cur
    coef_last_ref[...] = coef_prev

    @pl.when(e == ne - 1)
    def _():
        accumulate(vt_ref[:, (nsb - 1) * PEER_SB:], coef_last_ref[...], 0, tb)
        o_ref[...] = acc_ref[...].T


def _peer(h2, wqt, keys, u, vt):
    n, d = h2.shape
    tb = min(n, 512)
    hs = (PEER_HEADS, N_KEYS, tb)
    ne, nsb = N_EXPERTS // PEER_EB, PEER_EB // PEER_SB
    return pl.pallas_call(
        _peer_kernel,
        grid=(n // tb, ne),
        in_specs=[pl.BlockSpec((tb, d), lambda i, e: (i, 0)),
                  pl.BlockSpec(wqt.shape, lambda i, e: (0, 0)),
                  pl.BlockSpec(keys.shape, lambda i, e: (0, 0, 0)),
                  pl.BlockSpec((PEER_EB, d), lambda i, e: (e, 0)),
                  pl.BlockSpec((PEER_SB, d), lambda i, e: (jnp.minimum(e + 1, ne - 1) * nsb, 0)),
                  pl.BlockSpec((d, PEER_EB), lambda i, e: (0, e)),
                  pl.BlockSpec((d, PEER_SB), lambda i, e: (0, jnp.maximum(e * nsb - 1, 0)))],
        out_specs=pl.BlockSpec((tb, d), lambda i, e: (i, 0)),
        out_shape=jax.ShapeDtypeStruct((n, d), F32),
        scratch_shapes=[pltpu.VMEM((d, tb), BF16), pltpu.VMEM(hs, BF16), pltpu.VMEM(hs, jnp.uint32),
                        pltpu.VMEM(hs, jnp.uint32), pltpu.VMEM(hs, BF16), pltpu.VMEM((d, tb), F32),
                        pltpu.VMEM((PEER_SB, tb), F32), pltpu.VMEM((PEER_SB, tb), BF16),
                        pltpu.VMEM((2, PEER_TOPK + 1, SUBLANE, tb), F32)],
        compiler_params=_cparams(("arbitrary", "arbitrary")),
        name="peer",
    )(h2, wqt, keys, u, u, vt, vt)


def _final_kernel(x2_ref, p_ref, mod_ref, g_ref, o_ref, *, normalize):
    x = x2_ref[...] + mod_ref[...][5:6] * p_ref[...]
    if normalize:
        x = x * lax.rsqrt(jnp.mean(x * x, axis=-1, keepdims=True) + EPS) * g_ref[...]
    o_ref[...] = x


def _final(x2, peer_out, mod, g, normalize):
    b, t, d = x2.shape
    tm = min(t, 1024)
    spec = pl.BlockSpec((None, tm, d), lambda bi, i: (bi, i, 0))
    return pl.pallas_call(
        functools.partial(_final_kernel, normalize=normalize),
        grid=(b, t // tm),
        in_specs=[spec, spec,
                  pl.BlockSpec((None, 6, d), lambda bi, i: (bi, 0, 0)),
                  pl.BlockSpec((1, d), lambda bi, i: (0, 0))],
        out_specs=spec,
        out_shape=jax.ShapeDtypeStruct((b, t, d), F32),
        compiler_params=_cparams(("arbitrary", "arbitrary")),
        name="final_norm",
    )(x2, peer_out, mod, g.reshape(1, d))


def _prep_layer_weights(w_in, conv_w, conv_b, dt_bias, a_log, d_skip, ssm_norm_g, w_branch_ssm,
                        w_branch_attn, w_out, peer_wq, peer_keys, peer_u, peer_v):
    d = w_in.shape[0]
    off_xbc = D_INNER
    off_dt = off_xbc + CONV_DIM
    off_q = off_dt + SSM_HEADS
    off_k = off_q + ATTN_DIM
    off_v = off_k + KV_DIM
    off_gate = off_v + KV_DIM
    w_re = jnp.concatenate([
        w_in[:, off_xbc:off_dt], w_in[:, :off_xbc], w_in[:, off_gate:], w_in[:, off_q:off_k],
        w_in[:, off_k:off_v], w_in[:, off_v:off_gate]], axis=1).astype(BF16)
    w_dt = jnp.pad(w_in[:, off_dt:off_q], ((0, 0), (0, DT_PAD - SSM_HEADS))).astype(BF16)
    head_of_lane = jnp.arange(D_INNER) // SSM_HEADDIM
    e1 = (jnp.arange(SSM_HEADS)[:, None] == head_of_lane[None, :]).astype(BF16)
    return dict(
        w_re=w_re, w_dt=w_dt, conv_w=conv_w, conv_b=conv_b.reshape(1, CONV_DIM),
        dtb=dt_bias.reshape(1, SSM_HEADS), alog=a_log.reshape(1, SSM_HEADS),
        dsk_x=jnp.repeat(d_skip, SSM_HEADDIM).reshape(1, D_INNER),
        norm_g=ssm_norm_g.reshape(1, D_INNER), e3=jnp.concatenate([e1, e1, e1], axis=0),
        wbs=w_branch_ssm.astype(BF16), wba=w_branch_attn.astype(BF16), wo=w_out.astype(BF16),
        wqt=peer_wq.T.astype(BF16),
        keys=peer_keys.reshape(PEER_HEADS * 2, N_KEYS, D_HALF).astype(BF16),
        u=peer_u.astype(BF16), vt=peer_v.T.astype(BF16))


def _layer(x, mod, conv_hist, ssm0, k_hist, v_hist, norm1_g, norm2_g, sinks, lw):
    b, t, d = x.shape
    proj, dt_raw = _proj(x, mod, norm1_g, lw["w_re"], lw["w_dt"])
    hist8 = jnp.pad(conv_hist, ((0, 0), (SUBLANE - (CONV_W - 1), 0), (0, 0)))
    h0 = ssm0.reshape(b, SSM_GROUPS, GROUP_W, D_STATE)
    y_ssm, h_new = _ssd(proj, dt_raw, hist8, h0, lw["conv_w"], lw["conv_b"], lw["dtb"], lw["alog"],
                        lw["dsk_x"], lw["norm_g"], lw["e3"])
    if k_hist is not None:
        k_hist = k_hist.reshape(b, -1, KV_DIM)
        v_hist = v_hist.reshape(b, -1, KV_DIM)
    o_attn = _attn(proj, k_hist, v_hist, sinks)
    x2, h2 = _merge(x, y_ssm, o_attn, proj, mod, norm2_g, lw["wbs"], lw["wba"], lw["wo"])
    peer_out = _peer(h2.reshape(b * t, d), lw["wqt"], lw["keys"], lw["u"], lw["vt"]).reshape(b, t, d)
    keep = WINDOW if k_hist is None else t
    k_new = proj[:, t - keep:, COL_K:COL_K + KV_DIM].astype(F32).reshape(b, keep, N_KV, HEAD_DIM)
    v_new = proj[:, t - keep:, COL_V:COL_V + KV_DIM].astype(F32).reshape(b, keep, N_KV, HEAD_DIM)
    conv_new = proj[:, t - (CONV_W - 1):, COL_XBC:COL_XBC + CONV_DIM].astype(F32)
    ssm_new = h_new.reshape(b, SSM_HEADS, SSM_HEADDIM, D_STATE)
    return x2, peer_out, k_new, v_new, conv_new, ssm_new


def kernel(x_prompt, x_sample, cache_attn_k, cache_attn_v, state_conv, state_ssm, c_prompt, c_sample, ada_w, ada_b, norm1_g, w_in, conv_w, conv_b, dt_bias, a_log, d_skip, ssm_norm_g, attn_sinks, w_branch_ssm, w_branch_attn, w_out, norm2_g, peer_wq, peer_keys, peer_u, peer_v, final_g):
    depth = ada_w.shape[0]
    d = x_prompt.shape[-1]
    bp, bs = x_prompt.shape[0], x_sample.shape[0]
    xp, xs = x_prompt, x_sample
    c_all = jnp.concatenate([c_prompt, c_sample], axis=0)
    outs = [[] for _ in range(8)]
    for l in range(depth):
        lw = _prep_layer_weights(w_in[l], conv_w[l], conv_b[l], dt_bias[l], a_log[l], d_skip[l],
                                 ssm_norm_g[l], w_branch_ssm[l], w_branch_attn[l], w_out[l],
                                 peer_wq[l], peer_keys[l], peer_u[l], peer_v[l])
        mod = _adaln(c_all, ada_w[l], ada_b[l]).reshape(bp + bs, 6, d)
        last = l == depth - 1
        conv0 = jnp.zeros((bp, CONV_W - 1, CONV_DIM), F32)
        ssm0 = jnp.zeros((bp, SSM_HEADS, SSM_HEADDIM, D_STATE), F32)
        res = []
        for x, m, ch, s0, kh, vh in ((xp, mod[:bp], conv0, ssm0, None, None),
                                     (xs, mod[bp:], state_conv[l], state_ssm[l],
                                      cache_attn_k[l], cache_attn_v[l])):
            x2, peer_out, kn, vn, cn, sn = _layer(x, m, ch, s0, kh, vh, norm1_g[l], norm2_g[l],
                                                  attn_sinks[l], lw)
            res.append((_final(x2, peer_out, m, final_g, last), kn, vn, cn, sn))
        (xp, kp, vp, cp, sp), (xs, kn, vn, cn, sn) = res
        for lst, val in zip(outs, (kp, vp, cp, sp, kn, vn, cn, sn)):
            lst.append(val)
    return (xp, xs) + tuple(jnp.stack(o) for o in outs)
```

```python
import functools

import jax
import jax.numpy as jnp
from jax import lax
from jax.experimental import pallas as pl
from jax.experimental.pallas import tpu as pltpu

F32 = jnp.float32
BF16 = jnp.bfloat16

EPS = 1e-6
CHUNK = 64
WINDOW = 128
SSM_HEADS = 32
SSM_HEADDIM = 64
D_STATE = 128
SSM_GROUPS = 8
HEADS_PER_GROUP = SSM_HEADS // SSM_GROUPS
GROUP_W = HEADS_PER_GROUP * SSM_HEADDIM
D_INNER = SSM_HEADS * SSM_HEADDIM
CONV_W = 4
CONV_DIM = D_INNER + 2 * SSM_GROUPS * D_STATE
N_HEADS = 16
N_KV = 4
HEAD_DIM = 64
Q_PER_KV = N_HEADS // N_KV
ATTN_DIM = N_HEADS * HEAD_DIM
KV_DIM = N_KV * HEAD_DIM
ATTN_SCALE = HEAD_DIM ** -0.5
PEER_HEADS = 8
N_KEYS = 128
N_EXPERTS = N_KEYS * N_KEYS
PEER_TOPK = 16
D_HALF = 128

LANE = 128
SUBLANE = 8
DT_PAD = LANE

COL_XBC = 0
COL_Z = COL_XBC + CONV_DIM
COL_GATE = COL_Z + D_INNER
COL_Q = COL_GATE + 2048
COL_K = COL_Q + ATTN_DIM
COL_V = COL_K + KV_DIM
PROJ_W = COL_V + KV_DIM
PROJ_TN = 2432
CONV_CB = 1024

VMEM_LIMIT = 56 * 1024 * 1024


def _cparams(sem):
    return pltpu.CompilerParams(dimension_semantics=sem, vmem_limit_bytes=VMEM_LIMIT)


def _sigmoid(x):
    return 1.0 / (1.0 + jnp.exp(-x))


def _split2(v):
    hi = v.astype(BF16)
    lo = (v - hi.astype(F32)).astype(BF16)
    return hi, lo


def _split3(v):
    hi = v.astype(BF16)
    r = v - hi.astype(F32)
    mid = r.astype(BF16)
    lo = (r - mid.astype(F32)).astype(BF16)
    return hi, mid, lo


def _dot(a, b):
    return jnp.dot(a, b, preferred_element_type=F32)


def _dot_nt(a, b):
    return lax.dot_general(a, b, (((1,), (1,)), ((), ())), preferred_element_type=F32)


def _dot_tn(a, b):
    return lax.dot_general(a, b, (((0,), (0,)), ((), ())), preferred_element_type=F32)


def _adaln_kernel(c_ref, w_ref, b_ref, o_ref):
    c = c_ref[...]
    s = c * _sigmoid(c)
    s_hi, s_lo = _split2(s)
    w_hi, w_lo = _split2(w_ref[...])
    acc = _dot(s_hi, w_hi) + _dot(s_hi, w_lo) + _dot(s_lo, w_hi)
    o_ref[...] = acc + b_ref[...]


def _adaln(c, ada_w, ada_b):
    rows, d = c.shape
    n = ada_w.shape[1]
    tn = 1024
    return pl.pallas_call(
        _adaln_kernel,
        grid=(n // tn,),
        in_specs=[pl.BlockSpec((rows, d), lambda j: (0, 0)),
                  pl.BlockSpec((d, tn), lambda j: (0, j)),
                  pl.BlockSpec((1, tn), lambda j: (0, j))],
        out_specs=pl.BlockSpec((rows, tn), lambda j: (0, j)),
        out_shape=jax.ShapeDtypeStruct((rows, n), F32),
        compiler_params=_cparams(("arbitrary",)),
        name="adaln",
    )(c, ada_w, ada_b.reshape(1, n))


def _proj_kernel(x_ref, mod_ref, g_ref, w_ref, wdt_ref, o_ref, dt_ref, h_ref):
    @pl.when(pl.program_id(2) == 0)
    def _():
        x = x_ref[...]
        ms = jnp.mean(x * x, axis=-1, keepdims=True)
        xn = x * lax.rsqrt(ms + EPS) * g_ref[...]
        m = mod_ref[...]
        h_ref[...] = (xn * (1.0 + m[1:2]) + m[0:1]).astype(BF16)
        dt_ref[...] = _dot(h_ref[...], wdt_ref[...])

    o_ref[...] = _dot(h_ref[...], w_ref[...]).astype(o_ref.dtype)


def _proj(x, mod, g, w, w_dt):
    b, t, d = x.shape
    tm = min(t, 1024)
    return pl.pallas_call(
        _proj_kernel,
        grid=(b, t // tm, PROJ_W // PROJ_TN),
        in_specs=[pl.BlockSpec((None, tm, d), lambda bi, i, j: (bi, i, 0)),
                  pl.BlockSpec((None, 6, d), lambda bi, i, j: (bi, 0, 0)),
                  pl.BlockSpec((1, d), lambda bi, i, j: (0, 0)),
                  pl.BlockSpec((d, PROJ_TN), lambda bi, i, j: (0, j)),
                  pl.BlockSpec((d, DT_PAD), lambda bi, i, j: (0, 0))],
        out_specs=[pl.BlockSpec((None, tm, PROJ_TN), lambda bi, i, j: (bi, i, j)),
                   pl.BlockSpec((None, tm, DT_PAD), lambda bi, i, j: (bi, i, 0))],
        out_shape=[jax.ShapeDtypeStruct((b, t, PROJ_W), BF16),
                   jax.ShapeDtypeStruct((b, t, DT_PAD), F32)],
        scratch_shapes=[pltpu.VMEM((tm, d), BF16)],
        compiler_params=_cparams(("arbitrary", "arbitrary", "arbitrary")),
        name="norm_proj",
    )(x, mod, g.reshape(1, d), w, w_dt)


def _softplus(x):
    return jnp.maximum(x, 0.0) + jnp.log1p(jnp.exp(-jnp.abs(x)))


def _ssd_kernel(xbc_ref, z_ref, dt_ref, hist_ref, h0_ref, cw_ref, cb_ref, dtb_ref, alog_ref,
                dsk_ref, ng_ref, e3_ref, y_ref, hout_ref, hcat_ref, tail_ref, xc_ref, *, L):
    c = pl.program_id(1)
    nc = pl.num_programs(1)

    @pl.when(c == 0)
    def _():
        tail_ref[...] = hist_ref[...]
        for g in range(SSM_GROUPS):
            hcat_ref[g] = h0_ref[g].T

    row = lax.broadcasted_iota(jnp.int32, (L, L), 0)
    col = lax.broadcasted_iota(jnp.int32, (L, L), 1)
    causal = row >= col

    shifts = jnp.concatenate([jnp.where(row - col == k, 1.0, 0.0) for k in range(1, CONV_W)],
                             axis=0).astype(BF16)
    hp = tail_ref[...]
    row8 = lax.broadcasted_iota(jnp.int32, (SUBLANE, CONV_CB), 0)
    for cl in range(0, CONV_DIM, CONV_CB):
        ch = cl + CONV_CB
        xb = xbc_ref[:, cl:ch]
        x = xb.astype(F32)
        sh = _dot(shifts, xb)
        acc = cb_ref[:, cl:ch] + x * cw_ref[CONV_W - 1:CONV_W, cl:ch]
        for k in range(1, CONV_W):
            xr = sh[(k - 1) * L:k * L]
            top = jnp.where(row8 < k, pltpu.roll(hp[:, cl:ch], k, axis=0), xr[:SUBLANE])
            xr = jnp.concatenate([top, xr[SUBLANE:]], axis=0)
            acc = acc + xr * cw_ref[CONV_W - 1 - k:CONV_W - k, cl:ch]
        tail_ref[:, cl:ch] = x[L - SUBLANE:]
        xc_ref[:, cl:ch] = acc * _sigmoid(acc)

    dt = _softplus(dt_ref[...][:, :SSM_HEADS] + dtb_ref[...])
    a = dt * (-jnp.exp(alog_ref[...]))
    tril = jnp.where(causal, 1.0, 0.0).astype(BF16)
    cs = _dot(tril, jnp.concatenate(_split3(a), axis=1))
    acum = cs[:, :SSM_HEADS] + cs[:, SSM_HEADS:2 * SSM_HEADS] + cs[:, 2 * SSM_HEADS:]
    sel_r = lax.broadcasted_iota(jnp.int32, (SSM_HEADS, 3 * SSM_HEADS), 0)
    sel_c = lax.broadcasted_iota(jnp.int32, (SSM_HEADS, 3 * SSM_HEADS), 1)
    sel = jnp.where(sel_c % SSM_HEADS == sel_r, 1.0, 0.0).astype(BF16)
    acum_t = _dot_nt(sel, jnp.concatenate(_split3(acum), axis=1))
    a_last = acum[L - 1:L]
    fac = jnp.concatenate([dt, jnp.exp(acum), jnp.exp(a_last - acum)], axis=0)
    fac3 = jnp.concatenate(_split3(fac), axis=1)
    lane_head = lax.broadcasted_iota(jnp.int32, (L, GROUP_W), 1) // SSM_HEADDIM

    for g in range(SSM_GROUPS):
        lo, hi = g * GROUP_W, (g + 1) * GROUP_W
        xs_g = xc_ref[:, lo:hi]
        bm_g = xc_ref[:, D_INNER + g * D_STATE:D_INNER + (g + 1) * D_STATE].astype(BF16)
        cm_g = xc_ref[:, D_INNER + SSM_GROUPS * D_STATE + g * D_STATE:
                      D_INNER + SSM_GROUPS * D_STATE + (g + 1) * D_STATE].astype(BF16)
        fx = _dot(fac3, e3_ref[:, lo:hi])
        dt_x, ea_x, te_x = fx[:L], fx[L:2 * L], fx[2 * L:]
        xdt = xs_g * dt_x
        xdt_b = xdt.astype(BF16)
        cb = _dot_nt(cm_g, bm_g)
        ms = []
        for r in range(HEADS_PER_GROUP):
            h = g * HEADS_PER_GROUP + r
            seg = acum[:, h:h + 1] - acum_t[h:h + 1, :]
            dec = jnp.exp(jnp.where(causal, seg, -jnp.inf))
            ms.append((cb * dec).astype(BF16))
        yy = _dot(jnp.concatenate(ms, axis=0), xdt_b)
        y = yy[3 * L:]
        for r in range(HEADS_PER_GROUP - 2, -1, -1):
            y = jnp.where(lane_head == r, yy[r * L:(r + 1) * L], y)
        hc = hcat_ref[g]
        y = y + _dot(cm_g, hc.astype(BF16)) * ea_x
        hcat_ref[g] = ea_x[L - 1:L] * hc + _dot_tn(bm_g, (te_x * xdt).astype(BF16))
        y = y + dsk_ref[:, lo:hi] * xs_g
        zg = z_ref[:, lo:hi].astype(F32)
        y = y * (zg * _sigmoid(zg))
        y = y * lax.rsqrt(jnp.mean(y * y, axis=-1, keepdims=True) + EPS) * ng_ref[:, lo:hi]
        y_ref[:, lo:hi] = y.astype(y_ref.dtype)

    @pl.when(c == nc - 1)
    def _():
        for g in range(SSM_GROUPS):
            hout_ref[g] = hcat_ref[g].T


def _ssd(proj, dt_raw, hist8, h0, conv_w, conv_b, dtb, alog, dsk_x, norm_g, e3):
    b, t, _ = proj.shape
    L = min(t, 256)
    kern = functools.partial(_ssd_kernel, L=L)
    full = lambda shape: pl.BlockSpec(shape, lambda bi, ci: (0,) * len(shape))
    return pl.pallas_call(
        kern,
        grid=(b, t // L),
        in_specs=[pl.BlockSpec((None, L, CONV_DIM), lambda bi, ci: (bi, ci, COL_XBC // CONV_DIM)),
                  pl.BlockSpec((None, L, D_INNER), lambda bi, ci: (bi, ci, COL_Z // D_INNER)),
                  pl.BlockSpec((None, L, DT_PAD), lambda bi, ci: (bi, ci, 0)),
                  pl.BlockSpec((None, SUBLANE, CONV_DIM), lambda bi, ci: (bi, 0, 0)),
                  pl.BlockSpec((None, SSM_GROUPS, GROUP_W, D_STATE), lambda bi, ci: (bi, 0, 0, 0)),
                  full((CONV_W, CONV_DIM)), full((1, CONV_DIM)), full((1, SSM_HEADS)),
                  full((1, SSM_HEADS)), full((1, D_INNER)), full((1, D_INNER)),
                  full((3 * SSM_HEADS, D_INNER))],
        out_specs=[pl.BlockSpec((None, L, D_INNER), lambda bi, ci: (bi, ci, 0)),
                   pl.BlockSpec((None, SSM_GROUPS, GROUP_W, D_STATE), lambda bi, ci: (bi, 0, 0, 0))],
        out_shape=[jax.ShapeDtypeStruct((b, t, D_INNER), BF16),
                   jax.ShapeDtypeStruct((b, SSM_GROUPS, GROUP_W, D_STATE), F32)],
        scratch_shapes=[pltpu.VMEM((SSM_GROUPS, D_STATE, GROUP_W), F32),
                        pltpu.VMEM((SUBLANE, CONV_DIM), F32), pltpu.VMEM((L, CONV_DIM), F32)],
        compiler_params=_cparams(("arbitrary", "arbitrary")),
        name="ssd",
    )(proj, proj, dt_raw, hist8, h0, conv_w, conv_b, dtb, alog, dsk_x, norm_g, e3)


def _attn_kernel(sink_ref, q_ref, kp_ref, vp_ref, kc_ref, vc_ref, o_ref, *, tq, npast, banded):
    t = pl.program_id(1)
    cq = CHUNK if banded else tq
    nkw = npast + cq
    q = (q_ref[...] * ATTN_SCALE).astype(BF16)
    k = jnp.concatenate([kp_ref[...].astype(BF16), kc_ref[...].astype(BF16)], axis=0)
    v = jnp.concatenate([vp_ref[...].astype(BF16), vc_ref[...].astype(BF16)], axis=0)
    blocks = [(c, g) for c in range(tq // cq) for g in range(N_KV)]
    kj = lax.broadcasted_iota(jnp.int32, (1, nkw), 1)

    scores = []
    for c, g in blocks:
        qs = jnp.concatenate([q[c * cq:(c + 1) * cq, (g * Q_PER_KV + r) * HEAD_DIM:
                                (g * Q_PER_KV + r + 1) * HEAD_DIM] for r in range(Q_PER_KV)], axis=0)
        s = _dot_nt(qs, k[c * cq:c * cq + nkw, g * HEAD_DIM:(g + 1) * HEAD_DIM])
        if banded:
            s = s + jnp.where(t * tq + c * cq - npast + kj >= 0, 0.0, -jnp.inf)
        scores.append(s)
    sinks = [jnp.concatenate([jnp.full((cq, 1), sink_ref[g * Q_PER_KV + r], F32)
                              for r in range(Q_PER_KV)], axis=0) for g in range(N_KV)]
    ms = [jnp.maximum(jnp.max(s, axis=-1, keepdims=True), sinks[g]) for (c, g), s in zip(blocks, scores)]
    es = [jnp.exp(s - m) for s, m in zip(scores, ms)]
    dens = [jnp.sum(e, axis=-1, keepdims=True) for e in es]
    invs = [1.0 / (den + jnp.exp(sinks[g] - m)) for (c, g), den, m in zip(blocks, dens, ms)]
    outs = [_dot(e.astype(BF16), v[c * cq:c * cq + nkw, g * HEAD_DIM:(g + 1) * HEAD_DIM])
            for (c, g), e in zip(blocks, es)]
    outs = [o * inv for o, inv in zip(outs, invs)]
    for (c, g), o in zip(blocks, outs):
        for r in range(Q_PER_KV):
            h = g * Q_PER_KV + r
            o_ref[c * cq:(c + 1) * cq, h * HEAD_DIM:(h + 1) * HEAD_DIM] = (
                o[r * cq:(r + 1) * cq].astype(o_ref.dtype))


def _attn(proj, k_hist, v_hist, sinks):
    b, t, _ = proj.shape
    banded = k_hist is None
    tq = min(t, 256)
    npast = WINDOW
    kw, kb, vb = KV_DIM, COL_K // KV_DIM, COL_V // KV_DIM
    if banded:
        blocks_per_tile = tq // npast
        prev = lambda col: pl.BlockSpec(
            (None, npast, kw), lambda bi, ti: (bi, jnp.maximum(ti * blocks_per_tile - 1, 0), col))
        kp_spec, vp_spec, kp, vp = prev(kb), prev(vb), proj, proj
    else:
        kp_spec = vp_spec = pl.BlockSpec((None, npast, kw), lambda bi, ti: (bi, 0, 0))
        kp, vp = k_hist, v_hist
    kern = functools.partial(_attn_kernel, tq=tq, npast=npast, banded=banded)
    return pl.pallas_call(
        kern,
        grid=(b, t // tq),
        in_specs=[pl.BlockSpec(memory_space=pltpu.SMEM),
                  pl.BlockSpec((None, tq, ATTN_DIM), lambda bi, ti: (bi, ti, COL_Q // ATTN_DIM)),
                  kp_spec, vp_spec,
                  pl.BlockSpec((None, tq, kw), lambda bi, ti: (bi, ti, kb)),
                  pl.BlockSpec((None, tq, kw), lambda bi, ti: (bi, ti, vb))],
        out_specs=pl.BlockSpec((None, tq, ATTN_DIM), lambda bi, ti: (bi, ti, 0)),
        out_shape=jax.ShapeDtypeStruct((b, t, ATTN_DIM), BF16),
        compiler_params=_cparams(("arbitrary", "arbitrary")),
        name="attn",
    )(sinks, proj, kp, vp, proj, proj)


def _merge_kernel(x_ref, y_ref, o_ref, gate_ref, mod_ref, g2_ref, wbs_ref, wba_ref, wo_ref,
                  x2_ref, h2_ref):
    d = x_ref.shape[-1]
    a = _dot(y_ref[...], wbs_ref[...])
    b = _dot(o_ref[...], wba_ref[...])
    gt = gate_ref[...].astype(F32)
    merged = _sigmoid(gt[:, :d]) * a + _sigmoid(gt[:, d:]) * b
    m = mod_ref[...]
    x2 = x_ref[...] + m[2:3] * _dot(merged.astype(BF16), wo_ref[...])
    x2_ref[...] = x2
    xn = x2 * lax.rsqrt(jnp.mean(x2 * x2, axis=-1, keepdims=True) + EPS) * g2_ref[...]
    h2_ref[...] = (xn * (1.0 + m[4:5]) + m[3:4]).astype(h2_ref.dtype)


def _merge(x, y_ssm, o_attn, proj, mod, g2, wbs, wba, wo):
    b, t, d = x.shape
    tm = min(t, 512)
    full = lambda shape: pl.BlockSpec(shape, lambda bi, i: (0,) * len(shape))
    return pl.pallas_call(
        _merge_kernel,
        grid=(b, t // tm),
        in_specs=[pl.BlockSpec((None, tm, d), lambda bi, i: (bi, i, 0)),
                  pl.BlockSpec((None, tm, D_INNER), lambda bi, i: (bi, i, 0)),
                  pl.BlockSpec((None, tm, ATTN_DIM), lambda bi, i: (bi, i, 0)),
                  pl.BlockSpec((None, tm, 2 * d), lambda bi, i: (bi, i, COL_GATE // (2 * d))),
                  pl.BlockSpec((None, 6, d), lambda bi, i: (bi, 0, 0)),
                  full((1, d)), full((D_INNER, d)), full((ATTN_DIM, d)), full((d, d))],
        out_specs=[pl.BlockSpec((None, tm, d), lambda bi, i: (bi, i, 0)),
                   pl.BlockSpec((None, tm, d), lambda bi, i: (bi, i, 0))],
        out_shape=[jax.ShapeDtypeStruct((b, t, d), F32), jax.ShapeDtypeStruct((b, t, d), BF16)],
        compiler_params=_cparams(("arbitrary", "arbitrary")),
        name="merge",
    )(x, y_ssm, o_attn, proj, mod, g2.reshape(1, d), wbs, wba, wo)


def _batcher_pairs(n):
    pairs = []
    p = 1
    while p < n:
        k = p
        while k >= 1:
            for j in range(k % p, n - k, 2 * k):
                for i in range(min(k, n - j - k)):
                    if (i + j) // (2 * p) == (i + j + k) // (2 * p):
                        pairs.append((i + j, i + j + k))
            k //= 2
        p *= 2
    return tuple(pairs)


_SORT16 = _batcher_pairs(PEER_TOPK)


def _max2(a, b):
    return b if a is None else a if b is None else jnp.maximum(a, b)


def _min2(a, b):
    return None if a is None or b is None else jnp.minimum(a, b)


def _cmpx(rows, i, j):
    rows[i], rows[j] = _max2(rows[i], rows[j]), _min2(rows[i], rows[j])


def _top17(rows):
    rows = list(rows)
    for i, j in _SORT16:
        _cmpx(rows, i, j)
    x17 = None
    for sh in (4, 2, 1):
        roll = lambda a: None if a is None else pltpu.roll(a, sh, axis=0)
        other = [roll(rows[PEER_TOPK - 1 - r]) for r in range(PEER_TOPK)]
        lo = [_min2(rows[r], other[r]) for r in range(PEER_TOPK)]
        rows = [_max2(rows[r], other[r]) for r in range(PEER_TOPK)]
        x17 = functools.reduce(_max2, lo + [x17, roll(x17)])
        for dd in (8, 4, 2, 1):
            for i in range(PEER_TOPK):
                if i & dd == 0:
                    _cmpx(rows, i, i + dd)
    return rows, x17


def _select_threshold(v1, x1, v2, x2):
    n = v1[0].shape[1]
    sub = lax.broadcasted_iota(jnp.int32, (SUBLANE, n), 0)

    def pack(vals):
        out = vals[SUBLANE - 1]
        for b in range(SUBLANE - 2, -1, -1):
            out = jnp.where(sub == b, vals[b], out)
        return out

    v2lo, v2hi, v1hi = pack(v2[:SUBLANE]), pack(v2[SUBLANE:]), pack(v1[SUBLANE:])
    cands = ([v1[0] + v2lo, v1[0] + v2hi] + [v1[a] + v2lo for a in range(1, SUBLANE)]
             + [v1hi + v2[0]])
    c, c17 = _top17(cands + [None] * (PEER_TOPK - len(cands)))
    z = functools.reduce(jnp.add, [jnp.exp((ck - c[0])[0:1]) for ck in c])
    c17 = jnp.maximum(c17, jnp.maximum(x1 + v2[0], v1[0] + x2))
    return 0.5 * (c[PEER_TOPK - 1] + c17)[0:1], z


GELU_C0 = 0.7978845608028654
GELU_C1 = GELU_C0 * 0.044715


def _gelu_tanh_x2(x):
    return x * (1.0 + jnp.tanh(x * (GELU_C0 + GELU_C1 * (x * x))))


def _bf16_pair_words(x):
    bits = pltpu.bitcast(x.astype(BF16).astype(F32), jnp.uint32)
    return bits | (bits >> 16)


PEER_EB = 2048
PEER_SB = 512


def _peer_kernel(h2_ref, wqt_ref, keys_ref, u_first_ref, *refs):
    nsb = PEER_EB // PEER_SB
    u_refs, vt_refs = refs[:nsb], refs[nsb:2 * nsb]
    (vt_last_ref, o_ref, h2t_ref, rank_ref, cntw_ref, aiw_ref, bj_ref, acc_ref, s_first_ref,
     coef_last_ref, top_ref) = refs[2 * nsb:]
    e = pl.program_id(1)
    ne = pl.num_programs(1)
    tb = h2_ref.shape[0]

    @pl.when(e == 0)
    def _():
        h2t_ref[...] = h2_ref[...].astype(F32).T.astype(BF16)
        qt = _dot(wqt_ref[...], h2t_ref[...])
        for h in range(PEER_HEADS):
            q1 = qt[(2 * h) * D_HALF:(2 * h + 1) * D_HALF].astype(BF16)
            q2 = qt[(2 * h + 1) * D_HALF:(2 * h + 2) * D_HALF].astype(BF16)
            s1 = _dot(keys_ref[2 * h], q1)
            s2 = _dot(keys_ref[2 * h + 1], q2)
            for p, s in enumerate((s1, s2)):
                top, x17 = _top17([s[r * SUBLANE:(r + 1) * SUBLANE] for r in range(PEER_TOPK)])
                for r, row in enumerate(top + [x17]):
                    top_ref[p, r] = row
            v1 = [top_ref[0, r] for r in range(PEER_TOPK)]
            v2 = [top_ref[1, r] for r in range(PEER_TOPK)]
            tau, z = _select_threshold(v1, top_ref[0, PEER_TOPK], v2, top_ref[1, PEER_TOPK])
            thr = tau - s1
            rank = jnp.zeros_like(s2)
            cnt = jnp.zeros_like(s1)
            for r in range(PEER_TOPK):
                v2r = jnp.concatenate([v2[r]] * (N_KEYS // SUBLANE), axis=0)
                rank = jnp.where(v2r > s2, r + 1.0, rank)
                cnt = jnp.where(v2r >= thr, r + 1.0, cnt)
            rank_ref[h] = rank.astype(BF16)
            cntw_ref[h] = _bf16_pair_words(cnt)
            aiw_ref[h] = _bf16_pair_words(jnp.exp(s1 - v1[0][0:1]))
            bj_ref[h] = (jnp.exp(s2 - v2[0][0:1]) * (0.5 / z)).astype(BF16)
        acc_ref[...] = jnp.zeros_like(acc_ref)
        coef_last_ref[...] = jnp.zeros_like(coef_last_ref)
        s_first_ref[...] = _dot(u_first_ref[...], h2t_ref[...])

    nib = PEER_SB // N_KEYS
    halves = [(0, tb)] if tb < 4 * LANE else [(0, tb // 2), (tb // 2, tb)]

    def pre_acts(k, lo, hi):
        return _dot(u_refs[k - 1][...], h2t_ref[:, lo:hi])

    def coefficients(k, ii, s):
        i = e * (PEER_EB // N_KEYS) + k * nib + ii
        act = _gelu_tanh_x2(s[ii * N_KEYS:(ii + 1) * N_KEYS]).astype(BF16)
        w = None
        def row(ref, h):
            tile = pltpu.bitcast(jnp.broadcast_to(ref[h, pl.ds(i, 1), :], (SUBLANE, tb)), BF16)
            return jnp.concatenate([tile] * (N_KEYS // (2 * SUBLANE)), axis=0)

        for h in range(PEER_HEADS):
            hit = rank_ref[h] < row(cntw_ref, h)
            wh = jnp.where(hit, row(aiw_ref, h) * bj_ref[h], jnp.zeros((), BF16))
            w = wh if w is None else w + wh
        return w * act

    def accumulate(vt_sub, coef, lo, hi):
        acc_ref[:, lo:hi] += _dot(vt_sub, coef[:, lo:hi])

    s_cur = s_first_ref[...]
    coef_prev = coef_last_ref[...]
    for k in range(nsb):
        vt_prev = vt_refs[k][...]
        tasks = [(kind, lo, hi) for lo, hi in halves for kind in ("pre", "acc")]
        s_parts, coef = [], []
        for ii in range(nib):
            coef.append(coefficients(k, ii, s_cur))
            for kind, lo, hi in tasks[ii * len(tasks) // nib:(ii + 1) * len(tasks) // nib]:
                if kind == "pre":
                    s_parts.append(pre_acts(k + 1, lo, hi))
                else:
                    accumulate(vt_prev, coef_prev, lo, hi)
        s_cur = jnp.concatenate(s_parts, axis=1)
        coef_prev = jnp.concatenate(coef, axis=0)
    s_first_ref[...] = s_cur
    coef_last_ref[...] = coef_prev

    @pl.when(e == ne - 1)
    def _():
        accumulate(vt_last_ref[...], coef_last_ref[...], 0, tb)
        o_ref[...] = acc_ref[...].T


def _peer(h2, wqt, keys, u, vt):
    n, d = h2.shape
    tb = min(n, 512)
    hs = (PEER_HEADS, N_KEYS, tb)
    ne, nsb = N_EXPERTS // PEER_EB, PEER_EB // PEER_SB
    last = ne * nsb - 1
    u_spec = lambda j: pl.BlockSpec((PEER_SB, d), lambda i, e: (jnp.minimum(e * nsb + j, last), 0))
    vt_spec = lambda j: pl.BlockSpec((d, PEER_SB), lambda i, e: (0, jnp.maximum(e * nsb + j, 0)))
    return pl.pallas_call(
        _peer_kernel,
        grid=(n // tb, ne),
        in_specs=[pl.BlockSpec((tb, d), lambda i, e: (i, 0)),
                  pl.BlockSpec(wqt.shape, lambda i, e: (0, 0)),
                  pl.BlockSpec(keys.shape, lambda i, e: (0, 0, 0)),
                  pl.BlockSpec((PEER_SB, d), lambda i, e: (0, 0))]
                 + [u_spec(j) for j in range(1, nsb + 1)]
                 + [vt_spec(j) for j in range(-1, nsb - 1)]
                 + [pl.BlockSpec((d, PEER_SB), lambda i, e: (0, last))],
        out_specs=pl.BlockSpec((tb, d), lambda i, e: (i, 0)),
        out_shape=jax.ShapeDtypeStruct((n, d), F32),
        scratch_shapes=[pltpu.VMEM((d, tb), BF16), pltpu.VMEM(hs, BF16), pltpu.VMEM(hs, jnp.uint32),
                        pltpu.VMEM(hs, jnp.uint32), pltpu.VMEM(hs, BF16), pltpu.VMEM((d, tb), F32),
                        pltpu.VMEM((PEER_SB, tb), F32), pltpu.VMEM((PEER_SB, tb), BF16),
                        pltpu.VMEM((2, PEER_TOPK + 1, SUBLANE, tb), F32)],
        compiler_params=_cparams(("arbitrary", "arbitrary")),
        name="peer",
    )(h2, wqt, keys, u, *([u] * nsb), *([vt] * nsb), vt)


def _final_kernel(x2_ref, p_ref, mod_ref, g_ref, o_ref, *, normalize):
    x = x2_ref[...] + mod_ref[...][5:6] * p_ref[...]
    if normalize:
        x = x * lax.rsqrt(jnp.mean(x * x, axis=-1, keepdims=True) + EPS) * g_ref[...]
    o_ref[...] = x


def _final(x2, peer_out, mod, g, normalize):
    b, t, d = x2.shape
    tm = min(t, 1024)
    spec = pl.BlockSpec((None, tm, d), lambda bi, i: (bi, i, 0))
    return pl.pallas_call(
        functools.partial(_final_kernel, normalize=normalize),
        grid=(b, t // tm),
        in_specs=[spec, spec,
                  pl.BlockSpec((None, 6, d), lambda bi, i: (bi, 0, 0)),
                  pl.BlockSpec((1, d), lambda bi, i: (0, 0))],
        out_specs=spec,
        out_shape=jax.ShapeDtypeStruct((b, t, d), F32),
        compiler_params=_cparams(("arbitrary", "arbitrary")),
        name="final_norm",
    )(x2, peer_out, mod, g.reshape(1, d))


def _prep_layer_weights(w_in, conv_w, conv_b, dt_bias, a_log, d_skip, ssm_norm_g, w_branch_ssm,
                        w_branch_attn, w_out, peer_wq, peer_keys, peer_u, peer_v):
    d = w_in.shape[0]
    off_xbc = D_INNER
    off_dt = off_xbc + CONV_DIM
    off_q = off_dt + SSM_HEADS
    off_k = off_q + ATTN_DIM
    off_v = off_k + KV_DIM
    off_gate = off_v + KV_DIM
    w_re = jnp.concatenate([
        w_in[:, off_xbc:off_dt], w_in[:, :off_xbc], w_in[:, off_gate:], w_in[:, off_q:off_k],
        w_in[:, off_k:off_v], w_in[:, off_v:off_gate]], axis=1).astype(BF16)
    w_dt = jnp.pad(w_in[:, off_dt:off_q], ((0, 0), (0, DT_PAD - SSM_HEADS))).astype(BF16)
    head_of_lane = jnp.arange(D_INNER) // SSM_HEADDIM
    e1 = (jnp.arange(SSM_HEADS)[:, None] == head_of_lane[None, :]).astype(BF16)
    return dict(
        w_re=w_re, w_dt=w_dt, conv_w=conv_w, conv_b=conv_b.reshape(1, CONV_DIM),
        dtb=dt_bias.reshape(1, SSM_HEADS), alog=a_log.reshape(1, SSM_HEADS),
        dsk_x=jnp.repeat(d_skip, SSM_HEADDIM).reshape(1, D_INNER),
        norm_g=ssm_norm_g.reshape(1, D_INNER), e3=jnp.concatenate([e1, e1, e1], axis=0),
        wbs=w_branch_ssm.astype(BF16), wba=w_branch_attn.astype(BF16), wo=w_out.astype(BF16),
        wqt=peer_wq.T.astype(BF16),
        keys=peer_keys.reshape(PEER_HEADS * 2, N_KEYS, D_HALF).astype(BF16),
        u=peer_u.astype(BF16), vt=peer_v.T.astype(BF16))


def _layer(x, mod, conv_hist, ssm0, k_hist, v_hist, norm1_g, norm2_g, sinks, lw):
    b, t, d = x.shape
    proj, dt_raw = _proj(x, mod, norm1_g, lw["w_re"], lw["w_dt"])
    hist8 = jnp.pad(conv_hist, ((0, 0), (SUBLANE - (CONV_W - 1), 0), (0, 0)))
    h0 = ssm0.reshape(b, SSM_GROUPS, GROUP_W, D_STATE)
    y_ssm, h_new = _ssd(proj, dt_raw, hist8, h0, lw["conv_w"], lw["conv_b"], lw["dtb"], lw["alog"],
                        lw["dsk_x"], lw["norm_g"], lw["e3"])
    if k_hist is not None:
        k_hist = k_hist.reshape(b, -1, KV_DIM)
        v_hist = v_hist.reshape(b, -1, KV_DIM)
    o_attn = _attn(proj, k_hist, v_hist, sinks)
    x2, h2 = _merge(x, y_ssm, o_attn, proj, mod, norm2_g, lw["wbs"], lw["wba"], lw["wo"])
    peer_out = _peer(h2.reshape(b * t, d), lw["wqt"], lw["keys"], lw["u"], lw["vt"]).reshape(b, t, d)
    keep = WINDOW if k_hist is None else t
    k_new = proj[:, t - keep:, COL_K:COL_K + KV_DIM].astype(F32).reshape(b, keep, N_KV, HEAD_DIM)
    v_new = proj[:, t - keep:, COL_V:COL_V + KV_DIM].astype(F32).reshape(b, keep, N_KV, HEAD_DIM)
    conv_new = proj[:, t - (CONV_W - 1):, COL_XBC:COL_XBC + CONV_DIM].astype(F32)
    ssm_new = h_new.reshape(b, SSM_HEADS, SSM_HEADDIM, D_STATE)
    return x2, peer_out, k_new, v_new, conv_new, ssm_new


def kernel(x_prompt, x_sample, cache_attn_k, cache_attn_v, state_conv, state_ssm, c_prompt, c_sample, ada_w, ada_b, norm1_g, w_in, conv_w, conv_b, dt_bias, a_log, d_skip, ssm_norm_g, attn_sinks, w_branch_ssm, w_branch_attn, w_out, norm2_g, peer_wq, peer_keys, peer_u, peer_v, final_g):
    depth = ada_w.shape[0]
    d = x_prompt.shape[-1]
    bp, bs = x_prompt.shape[0], x_sample.shape[0]
    xp, xs = x_prompt, x_sample
    c_all = jnp.concatenate([c_prompt, c_sample], axis=0)
    outs = [[] for _ in range(8)]
    for l in range(depth):
        lw = _prep_layer_weights(w_in[l], conv_w[l], conv_b[l], dt_bias[l], a_log[l], d_skip[l],
                                 ssm_norm_g[l], w_branch_ssm[l], w_branch_attn[l], w_out[l],
                                 peer_wq[l], peer_keys[l], peer_u[l], peer_v[l])
        mod = _adaln(c_all, ada_w[l], ada_b[l]).reshape(bp + bs, 6, d)
        last = l == depth - 1
        conv0 = jnp.zeros((bp, CONV_W - 1, CONV_DIM), F32)
        ssm0 = jnp.zeros((bp, SSM_HEADS, SSM_HEADDIM, D_STATE), F32)
        res = []
        for x, m, ch, s0, kh, vh in ((xp, mod[:bp], conv0, ssm0, None, None),
                                     (xs, mod[bp:], state_conv[l], state_ssm[l],
                                      cache_attn_k[l], cache_attn_v[l])):
            x2, peer_out, kn, vn, cn, sn = _layer(x, m, ch, s0, kh, vh, norm1_g[l], norm2_g[l],
                                                  attn_sinks[l], lw)
            res.append((_final(x2, peer_out, m, final_g, last), kn, vn, cn, sn))
        (xp, kp, vp, cp, sp), (xs, kn, vn, cn, sn) = res
        for lst, val in zip(outs, (kp, vp, cp, sp, kn, vn, cn, sn)):
            lst.append(val)
    return (xp, xs) + tuple(jnp.stack(o) for o in outs)
```

```python
import functools

import jax
import jax.numpy as jnp
from jax import lax
from jax.experimental import pallas as pl
from jax.experimental.pallas import tpu as pltpu

F32 = jnp.float32
BF16 = jnp.bfloat16

EPS = 1e-6
CHUNK = 64
WINDOW = 128
SSM_HEADS = 32
SSM_HEADDIM = 64
D_STATE = 128
SSM_GROUPS = 8
HEADS_PER_GROUP = SSM_HEADS // SSM_GROUPS
GROUP_W = HEADS_PER_GROUP * SSM_HEADDIM
D_INNER = SSM_HEADS * SSM_HEADDIM
CONV_W = 4
CONV_DIM = D_INNER + 2 * SSM_GROUPS * D_STATE
N_HEADS = 16
N_KV = 4
HEAD_DIM = 64
Q_PER_KV = N_HEADS // N_KV
ATTN_DIM = N_HEADS * HEAD_DIM
KV_DIM = N_KV * HEAD_DIM
ATTN_SCALE = HEAD_DIM ** -0.5
PEER_HEADS = 8
N_KEYS = 128
N_EXPERTS = N_KEYS * N_KEYS
PEER_TOPK = 16
D_HALF = 128

LANE = 128
SUBLANE = 8
DT_PAD = LANE

COL_XBC = 0
COL_Z = COL_XBC + CONV_DIM
COL_GATE = COL_Z + D_INNER
COL_Q = COL_GATE + 2048
COL_K = COL_Q + ATTN_DIM
COL_V = COL_K + KV_DIM
PROJ_W = COL_V + KV_DIM
PROJ_TN = 2432
CONV_CB = 1024

VMEM_LIMIT = 56 * 1024 * 1024


def _cparams(sem):
    return pltpu.CompilerParams(dimension_semantics=sem, vmem_limit_bytes=VMEM_LIMIT)


def _sigmoid(x):
    return 1.0 / (1.0 + jnp.exp(-x))


def _split2(v):
    hi = v.astype(BF16)
    lo = (v - hi.astype(F32)).astype(BF16)
    return hi, lo


def _split3(v):
    hi = v.astype(BF16)
    r = v - hi.astype(F32)
    mid = r.astype(BF16)
    lo = (r - mid.astype(F32)).astype(BF16)
    return hi, mid, lo


def _dot(a, b):
    return jnp.dot(a, b, preferred_element_type=F32)


def _dot_nt(a, b):
    return lax.dot_general(a, b, (((1,), (1,)), ((), ())), preferred_element_type=F32)


def _dot_tn(a, b):
    return lax.dot_general(a, b, (((0,), (0,)), ((), ())), preferred_element_type=F32)


def _adaln_kernel(c_ref, w_ref, b_ref, o_ref):
    c = c_ref[...]
    s = c * _sigmoid(c)
    s_hi, s_lo = _split2(s)
    w_hi, w_lo = _split2(w_ref[...])
    acc = _dot(s_hi, w_hi) + _dot(s_hi, w_lo) + _dot(s_lo, w_hi)
    o_ref[...] = acc + b_ref[...]


def _adaln(c, ada_w, ada_b):
    rows, d = c.shape
    n = ada_w.shape[1]
    tn = 1024
    return pl.pallas_call(
        _adaln_kernel,
        grid=(n // tn,),
        in_specs=[pl.BlockSpec((rows, d), lambda j: (0, 0)),
                  pl.BlockSpec((d, tn), lambda j: (0, j)),
                  pl.BlockSpec((1, tn), lambda j: (0, j))],
        out_specs=pl.BlockSpec((rows, tn), lambda j: (0, j)),
        out_shape=jax.ShapeDtypeStruct((rows, n), F32),
        compiler_params=_cparams(("arbitrary",)),
        name="adaln",
    )(c, ada_w, ada_b.reshape(1, n))


def _proj_kernel(x_ref, mod_ref, g_ref, w_ref, wdt_ref, o_ref, dt_ref, h_ref):
    @pl.when(pl.program_id(2) == 0)
    def _():
        x = x_ref[...]
        ms = jnp.mean(x * x, axis=-1, keepdims=True)
        xn = x * lax.rsqrt(ms + EPS) * g_ref[...]
        m = mod_ref[...]
        h_ref[...] = (xn * (1.0 + m[1:2]) + m[0:1]).astype(BF16)
        dt_ref[...] = _dot(h_ref[...], wdt_ref[...])

    o_ref[...] = _dot(h_ref[...], w_ref[...]).astype(o_ref.dtype)


def _proj(x, mod, g, w, w_dt):
    b, t, d = x.shape
    tm = min(t, 1024)
    return pl.pallas_call(
        _proj_kernel,
        grid=(b, t // tm, PROJ_W // PROJ_TN),
        in_specs=[pl.BlockSpec((None, tm, d), lambda bi, i, j: (bi, i, 0)),
                  pl.BlockSpec((None, 6, d), lambda bi, i, j: (bi, 0, 0)),
                  pl.BlockSpec((1, d), lambda bi, i, j: (0, 0)),
                  pl.BlockSpec((d, PROJ_TN), lambda bi, i, j: (0, j)),
                  pl.BlockSpec((d, DT_PAD), lambda bi, i, j: (0, 0))],
        out_specs=[pl.BlockSpec((None, tm, PROJ_TN), lambda bi, i, j: (bi, i, j)),
                   pl.BlockSpec((None, tm, DT_PAD), lambda bi, i, j: (bi, i, 0))],
        out_shape=[jax.ShapeDtypeStruct((b, t, PROJ_W), BF16),
                   jax.ShapeDtypeStruct((b, t, DT_PAD), F32)],
        scratch_shapes=[pltpu.VMEM((tm, d), BF16)],
        compiler_params=_cparams(("arbitrary", "arbitrary", "arbitrary")),
        name="norm_proj",
    )(x, mod, g.reshape(1, d), w, w_dt)


def _softplus(x):
    return jnp.maximum(x, 0.0) + jnp.log1p(jnp.exp(-jnp.abs(x)))


def _ssd_kernel(xbc_ref, z_ref, dt_ref, hist_ref, h0_ref, cw_ref, cb_ref, dtb_ref, alog_ref,
                dsk_ref, ng_ref, e3_ref, y_ref, hout_ref, hcat_ref, tail_ref, xc_ref, *, L):
    c = pl.program_id(1)
    nc = pl.num_programs(1)

    @pl.when(c == 0)
    def _():
        tail_ref[...] = hist_ref[...]
        for g in range(SSM_GROUPS):
            hcat_ref[g] = h0_ref[g].T

    row = lax.broadcasted_iota(jnp.int32, (L, L), 0)
    col = lax.broadcasted_iota(jnp.int32, (L, L), 1)
    causal = row >= col

    shifts = jnp.concatenate([jnp.where(row - col == k, 1.0, 0.0) for k in range(1, CONV_W)],
                             axis=0).astype(BF16)
    hp = tail_ref[...]
    row8 = lax.broadcasted_iota(jnp.int32, (SUBLANE, CONV_CB), 0)
    for cl in range(0, CONV_DIM, CONV_CB):
        ch = cl + CONV_CB
        xb = xbc_ref[:, cl:ch]
        x = xb.astype(F32)
        sh = _dot(shifts, xb)
        acc = cb_ref[:, cl:ch] + x * cw_ref[CONV_W - 1:CONV_W, cl:ch]
        for k in range(1, CONV_W):
            xr = sh[(k - 1) * L:k * L]
            top = jnp.where(row8 < k, pltpu.roll(hp[:, cl:ch], k, axis=0), xr[:SUBLANE])
            xr = jnp.concatenate([top, xr[SUBLANE:]], axis=0)
            acc = acc + xr * cw_ref[CONV_W - 1 - k:CONV_W - k, cl:ch]
        tail_ref[:, cl:ch] = x[L - SUBLANE:]
        xc_ref[:, cl:ch] = acc * _sigmoid(acc)

    dt = _softplus(dt_ref[...][:, :SSM_HEADS] + dtb_ref[...])
    a = dt * (-jnp.exp(alog_ref[...]))
    tril = jnp.where(causal, 1.0, 0.0).astype(BF16)
    cs = _dot(tril, jnp.concatenate(_split3(a), axis=1))
    acum = cs[:, :SSM_HEADS] + cs[:, SSM_HEADS:2 * SSM_HEADS] + cs[:, 2 * SSM_HEADS:]
    sel_r = lax.broadcasted_iota(jnp.int32, (SSM_HEADS, 3 * SSM_HEADS), 0)
    sel_c = lax.broadcasted_iota(jnp.int32, (SSM_HEADS, 3 * SSM_HEADS), 1)
    sel = jnp.where(sel_c % SSM_HEADS == sel_r, 1.0, 0.0).astype(BF16)
    acum_t = _dot_nt(sel, jnp.concatenate(_split3(acum), axis=1))
    a_last = acum[L - 1:L]
    fac = jnp.concatenate([dt, jnp.exp(acum), jnp.exp(a_last - acum)], axis=0)
    fac3 = jnp.concatenate(_split3(fac), axis=1)
    lane_head = lax.broadcasted_iota(jnp.int32, (L, GROUP_W), 1) // SSM_HEADDIM

    for g in range(SSM_GROUPS):
        lo, hi = g * GROUP_W, (g + 1) * GROUP_W
        xs_g = xc_ref[:, lo:hi]
        bm_g = xc_ref[:, D_INNER + g * D_STATE:D_INNER + (g + 1) * D_STATE].astype(BF16)
        cm_g = xc_ref[:, D_INNER + SSM_GROUPS * D_STATE + g * D_STATE:
                      D_INNER + SSM_GROUPS * D_STATE + (g + 1) * D_STATE].astype(BF16)
        fx = _dot(fac3, e3_ref[:, lo:hi])
        dt_x, ea_x, te_x = fx[:L], fx[L:2 * L], fx[2 * L:]
        xdt = xs_g * dt_x
        xdt_b = xdt.astype(BF16)
        cb = _dot_nt(cm_g, bm_g)
        ms = []
        for r in range(HEADS_PER_GROUP):
            h = g * HEADS_PER_GROUP + r
            seg = acum[:, h:h + 1] - acum_t[h:h + 1, :]
            dec = jnp.exp(jnp.where(causal, seg, -jnp.inf))
            ms.append((cb * dec).astype(BF16))
        yy = _dot(jnp.concatenate(ms, axis=0), xdt_b)
        y = yy[3 * L:]
        for r in range(HEADS_PER_GROUP - 2, -1, -1):
            y = jnp.where(lane_head == r, yy[r * L:(r + 1) * L], y)
        hc = hcat_ref[g]
        y = y + _dot(cm_g, hc.astype(BF16)) * ea_x
        hcat_ref[g] = ea_x[L - 1:L] * hc + _dot_tn(bm_g, (te_x * xdt).astype(BF16))
        y = y + dsk_ref[:, lo:hi] * xs_g
        zg = z_ref[:, lo:hi].astype(F32)
        y = y * (zg * _sigmoid(zg))
        y = y * lax.rsqrt(jnp.mean(y * y, axis=-1, keepdims=True) + EPS) * ng_ref[:, lo:hi]
        y_ref[:, lo:hi] = y.astype(y_ref.dtype)

    @pl.when(c == nc - 1)
    def _():
        for g in range(SSM_GROUPS):
            hout_ref[g] = hcat_ref[g].T


def _ssd(proj, dt_raw, hist8, h0, conv_w, conv_b, dtb, alog, dsk_x, norm_g, e3):
    b, t, _ = proj.shape
    L = min(t, 256)
    kern = functools.partial(_ssd_kernel, L=L)
    full = lambda shape: pl.BlockSpec(shape, lambda bi, ci: (0,) * len(shape))
    return pl.pallas_call(
        kern,
        grid=(b, t // L),
        in_specs=[pl.BlockSpec((None, L, CONV_DIM), lambda bi, ci: (bi, ci, COL_XBC // CONV_DIM)),
                  pl.BlockSpec((None, L, D_INNER), lambda bi, ci: (bi, ci, COL_Z // D_INNER)),
                  pl.BlockSpec((None, L, DT_PAD), lambda bi, ci: (bi, ci, 0)),
                  pl.BlockSpec((None, SUBLANE, CONV_DIM), lambda bi, ci: (bi, 0, 0)),
                  pl.BlockSpec((None, SSM_GROUPS, GROUP_W, D_STATE), lambda bi, ci: (bi, 0, 0, 0)),
                  full((CONV_W, CONV_DIM)), full((1, CONV_DIM)), full((1, SSM_HEADS)),
                  full((1, SSM_HEADS)), full((1, D_INNER)), full((1, D_INNER)),
                  full((3 * SSM_HEADS, D_INNER))],
        out_specs=[pl.BlockSpec((None, L, D_INNER), lambda bi, ci: (bi, ci, 0)),
                   pl.BlockSpec((None, SSM_GROUPS, GROUP_W, D_STATE), lambda bi, ci: (bi, 0, 0, 0))],
        out_shape=[jax.ShapeDtypeStruct((b, t, D_INNER), BF16),
                   jax.ShapeDtypeStruct((b, SSM_GROUPS, GROUP_W, D_STATE), F32)],
        scratch_shapes=[pltpu.VMEM((SSM_GROUPS, D_STATE, GROUP_W), F32),
                        pltpu.VMEM((SUBLANE, CONV_DIM), F32), pltpu.VMEM((L, CONV_DIM), F32)],
        compiler_params=_cparams(("arbitrary", "arbitrary")),
        name="ssd",
    )(proj, proj, dt_raw, hist8, h0, conv_w, conv_b, dtb, alog, dsk_x, norm_g, e3)


def _attn_kernel(sink_ref, q_ref, kp_ref, vp_ref, kc_ref, vc_ref, o_ref, *, tq, npast, banded):
    t = pl.program_id(1)
    cq = CHUNK if banded else tq
    nkw = npast + cq
    q = (q_ref[...] * ATTN_SCALE).astype(BF16)
    k = jnp.concatenate([kp_ref[...].astype(BF16), kc_ref[...].astype(BF16)], axis=0)
    v = jnp.concatenate([vp_ref[...].astype(BF16), vc_ref[...].astype(BF16)], axis=0)
    blocks = [(c, g) for c in range(tq // cq) for g in range(N_KV)]
    kj = lax.broadcasted_iota(jnp.int32, (1, nkw), 1)

    scores = []
    for c, g in blocks:
        qs = jnp.concatenate([q[c * cq:(c + 1) * cq, (g * Q_PER_KV + r) * HEAD_DIM:
                                (g * Q_PER_KV + r + 1) * HEAD_DIM] for r in range(Q_PER_KV)], axis=0)
        s = _dot_nt(qs, k[c * cq:c * cq + nkw, g * HEAD_DIM:(g + 1) * HEAD_DIM])
        if banded:
            s = s + jnp.where(t * tq + c * cq - npast + kj >= 0, 0.0, -jnp.inf)
        scores.append(s)
    sinks = [jnp.concatenate([jnp.full((cq, 1), sink_ref[g * Q_PER_KV + r], F32)
                              for r in range(Q_PER_KV)], axis=0) for g in range(N_KV)]
    ms = [jnp.maximum(jnp.max(s, axis=-1, keepdims=True), sinks[g]) for (c, g), s in zip(blocks, scores)]
    es = [jnp.exp(s - m) for s, m in zip(scores, ms)]
    dens = [jnp.sum(e, axis=-1, keepdims=True) for e in es]
    invs = [1.0 / (den + jnp.exp(sinks[g] - m)) for (c, g), den, m in zip(blocks, dens, ms)]
    outs = [_dot(e.astype(BF16), v[c * cq:c * cq + nkw, g * HEAD_DIM:(g + 1) * HEAD_DIM])
            for (c, g), e in zip(blocks, es)]
    outs = [o * inv for o, inv in zip(outs, invs)]
    for (c, g), o in zip(blocks, outs):
        for r in range(Q_PER_KV):
            h = g * Q_PER_KV + r
            o_ref[c * cq:(c + 1) * cq, h * HEAD_DIM:(h + 1) * HEAD_DIM] = (
                o[r * cq:(r + 1) * cq].astype(o_ref.dtype))


def _attn(proj, k_hist, v_hist, sinks):
    b, t, _ = proj.shape
    banded = k_hist is None
    tq = min(t, 256)
    npast = WINDOW
    kw, kb, vb = KV_DIM, COL_K // KV_DIM, COL_V // KV_DIM
    if banded:
        blocks_per_tile = tq // npast
        prev = lambda col: pl.BlockSpec(
            (None, npast, kw), lambda bi, ti: (bi, jnp.maximum(ti * blocks_per_tile - 1, 0), col))
        kp_spec, vp_spec, kp, vp = prev(kb), prev(vb), proj, proj
    else:
        kp_spec = vp_spec = pl.BlockSpec((None, npast, kw), lambda bi, ti: (bi, 0, 0))
        kp, vp = k_hist, v_hist
    kern = functools.partial(_attn_kernel, tq=tq, npast=npast, banded=banded)
    return pl.pallas_call(
        kern,
        grid=(b, t // tq),
        in_specs=[pl.BlockSpec(memory_space=pltpu.SMEM),
                  pl.BlockSpec((None, tq, ATTN_DIM), lambda bi, ti: (bi, ti, COL_Q // ATTN_DIM)),
                  kp_spec, vp_spec,
                  pl.BlockSpec((None, tq, kw), lambda bi, ti: (bi, ti, kb)),
                  pl.BlockSpec((None, tq, kw), lambda bi, ti: (bi, ti, vb))],
        out_specs=pl.BlockSpec((None, tq, ATTN_DIM), lambda bi, ti: (bi, ti, 0)),
        out_shape=jax.ShapeDtypeStruct((b, t, ATTN_DIM), BF16),
        compiler_params=_cparams(("arbitrary", "arbitrary")),
        name="attn",
    )(sinks, proj, kp, vp, proj, proj)


def _merge_kernel(x_ref, y_ref, o_ref, gate_ref, mod_ref, g2_ref, wbs_ref, wba_ref, wo_ref,
                  x2_ref, h2_ref):
    d = x_ref.shape[-1]
    a = _dot(y_ref[...], wbs_ref[...])
    b = _dot(o_ref[...], wba_ref[...])
    gt = gate_ref[...].astype(F32)
    merged = _sigmoid(gt[:, :d]) * a + _sigmoid(gt[:, d:]) * b
    m = mod_ref[...]
    x2 = x_ref[...] + m[2:3] * _dot(merged.astype(BF16), wo_ref[...])
    x2_ref[...] = x2
    xn = x2 * lax.rsqrt(jnp.mean(x2 * x2, axis=-1, keepdims=True) + EPS) * g2_ref[...]
    h2_ref[...] = (xn * (1.0 + m[4:5]) + m[3:4]).astype(h2_ref.dtype)


def _merge(x, y_ssm, o_attn, proj, mod, g2, wbs, wba, wo):
    b, t, d = x.shape
    tm = min(t, 512)
    full = lambda shape: pl.BlockSpec(shape, lambda bi, i: (0,) * len(shape))
    return pl.pallas_call(
        _merge_kernel,
        grid=(b, t // tm),
        in_specs=[pl.BlockSpec((None, tm, d), lambda bi, i: (bi, i, 0)),
                  pl.BlockSpec((None, tm, D_INNER), lambda bi, i: (bi, i, 0)),
                  pl.BlockSpec((None, tm, ATTN_DIM), lambda bi, i: (bi, i, 0)),
                  pl.BlockSpec((None, tm, 2 * d), lambda bi, i: (bi, i, COL_GATE // (2 * d))),
                  pl.BlockSpec((None, 6, d), lambda bi, i: (bi, 0, 0)),
                  full((1, d)), full((D_INNER, d)), full((ATTN_DIM, d)), full((d, d))],
        out_specs=[pl.BlockSpec((None, tm, d), lambda bi, i: (bi, i, 0)),
                   pl.BlockSpec((None, tm, d), lambda bi, i: (bi, i, 0))],
        out_shape=[jax.ShapeDtypeStruct((b, t, d), F32), jax.ShapeDtypeStruct((b, t, d), BF16)],
        compiler_params=_cparams(("arbitrary", "arbitrary")),
        name="merge",
    )(x, y_ssm, o_attn, proj, mod, g2.reshape(1, d), wbs, wba, wo)


def _batcher_pairs(n):
    pairs = []
    p = 1
    while p < n:
        k = p
        while k >= 1:
            for j in range(k % p, n - k, 2 * k):
                for i in range(min(k, n - j - k)):
                    if (i + j) // (2 * p) == (i + j + k) // (2 * p):
                        pairs.append((i + j, i + j + k))
            k //= 2
        p *= 2
    return tuple(pairs)


_SORT16 = _batcher_pairs(PEER_TOPK)


def _max2(a, b):
    return b if a is None else a if b is None else jnp.maximum(a, b)


def _min2(a, b):
    return None if a is None or b is None else jnp.minimum(a, b)


def _cmpx(rows, i, j):
    rows[i], rows[j] = _max2(rows[i], rows[j]), _min2(rows[i], rows[j])


def _top17(rows):
    rows = list(rows)
    for i, j in _SORT16:
        _cmpx(rows, i, j)
    x17 = None
    for sh in (4, 2, 1):
        roll = lambda a: None if a is None else pltpu.roll(a, sh, axis=0)
        other = [roll(rows[PEER_TOPK - 1 - r]) for r in range(PEER_TOPK)]
        lo = [_min2(rows[r], other[r]) for r in range(PEER_TOPK)]
        rows = [_max2(rows[r], other[r]) for r in range(PEER_TOPK)]
        x17 = functools.reduce(_max2, lo + [x17, roll(x17)])
        for dd in (8, 4, 2, 1):
            for i in range(PEER_TOPK):
                if i & dd == 0:
                    _cmpx(rows, i, i + dd)
    return rows, x17


def _select_threshold(v1, x1, v2, x2):
    n = v1[0].shape[1]
    sub = lax.broadcasted_iota(jnp.int32, (SUBLANE, n), 0)

    def pack(vals):
        out = vals[SUBLANE - 1]
        for b in range(SUBLANE - 2, -1, -1):
            out = jnp.where(sub == b, vals[b], out)
        return out

    v2lo, v2hi, v1hi = pack(v2[:SUBLANE]), pack(v2[SUBLANE:]), pack(v1[SUBLANE:])
    cands = ([v1[0] + v2lo, v1[0] + v2hi] + [v1[a] + v2lo for a in range(1, SUBLANE)]
             + [v1hi + v2[0]])
    c, c17 = _top17(cands + [None] * (PEER_TOPK - len(cands)))
    z = functools.reduce(jnp.add, [jnp.exp((ck - c[0])[0:1]) for ck in c])
    c17 = jnp.maximum(c17, jnp.maximum(x1 + v2[0], v1[0] + x2))
    return 0.5 * (c[PEER_TOPK - 1] + c17)[0:1], z


GELU_C0 = 0.7978845608028654
GELU_C1 = GELU_C0 * 0.044715


def _gelu_tanh_x2(x):
    return x * (1.0 + jnp.tanh(x * (GELU_C0 + GELU_C1 * (x * x))))


def _bf16_pair_words(x):
    bits = pltpu.bitcast(x.astype(BF16).astype(F32), jnp.uint32)
    return bits | (bits >> 16)


PEER_EB = 2048
PEER_SB = 512


def _peer_kernel(h2_ref, wqt_ref, keys_ref, u_first_ref, *refs):
    nsb = PEER_EB // PEER_SB
    u_refs, vt_refs = refs[:nsb], refs[nsb:2 * nsb]
    (vt_last_ref, o_ref, h2t_ref, rank_ref, cntw_ref, aiw_ref, bj_ref, acc_ref, s_first_ref,
     coef_last_ref, top_ref) = refs[2 * nsb:]
    e = pl.program_id(1)
    ne = pl.num_programs(1)
    tb = h2_ref.shape[0]

    @pl.when(e == 0)
    def _():
        h2t_ref[...] = h2_ref[...].astype(F32).T.astype(BF16)
        qt = _dot(wqt_ref[...], h2t_ref[...])
        for h in range(PEER_HEADS):
            q1 = qt[(2 * h) * D_HALF:(2 * h + 1) * D_HALF].astype(BF16)
            q2 = qt[(2 * h + 1) * D_HALF:(2 * h + 2) * D_HALF].astype(BF16)
            s1 = _dot(keys_ref[2 * h], q1)
            s2 = _dot(keys_ref[2 * h + 1], q2)
            for p, s in enumerate((s1, s2)):
                top, x17 = _top17([s[r * SUBLANE:(r + 1) * SUBLANE] for r in range(PEER_TOPK)])
                for r, row in enumerate(top + [x17]):
                    top_ref[p, r] = row
            v1 = [top_ref[0, r] for r in range(PEER_TOPK)]
            v2 = [top_ref[1, r] for r in range(PEER_TOPK)]
            tau, z = _select_threshold(v1, top_ref[0, PEER_TOPK], v2, top_ref[1, PEER_TOPK])
            thr = tau - s1
            rank = jnp.zeros_like(s2)
            cnt = jnp.zeros_like(s1)
            for r in range(PEER_TOPK):
                v2r = jnp.concatenate([v2[r]] * (N_KEYS // SUBLANE), axis=0)
                rank = jnp.where(v2r > s2, r + 1.0, rank)
                cnt = jnp.where(v2r >= thr, r + 1.0, cnt)
            rank_ref[h] = rank.astype(BF16)
            cntw_ref[h] = _bf16_pair_words(cnt)
            aiw_ref[h] = _bf16_pair_words(jnp.exp(s1 - v1[0][0:1]))
            bj_ref[h] = (jnp.exp(s2 - v2[0][0:1]) * (0.5 / z)).astype(BF16)
        acc_ref[...] = jnp.zeros_like(acc_ref)
        coef_last_ref[...] = jnp.zeros_like(coef_last_ref)
        s_first_ref[...] = _dot(u_first_ref[...], h2t_ref[...])

    nib = PEER_SB // N_KEYS
    halves = [(0, tb)] if tb < 4 * LANE else [(0, tb // 2), (tb // 2, tb)]

    def pre_acts(k, lo, hi):
        return _dot(u_refs[k - 1][...], h2t_ref[:, lo:hi])

    def coefficients(k, ii, s):
        i = e * (PEER_EB // N_KEYS) + k * nib + ii
        act = _gelu_tanh_x2(s[ii * N_KEYS:(ii + 1) * N_KEYS]).astype(BF16)
        w = None
        def row(ref, h):
            tile = pltpu.bitcast(jnp.broadcast_to(ref[h, pl.ds(i, 1), :], (SUBLANE, tb)), BF16)
            return jnp.concatenate([tile] * (N_KEYS // (2 * SUBLANE)), axis=0)

        for h in range(PEER_HEADS):
            hit = rank_ref[h] < row(cntw_ref, h)
            wh = jnp.where(hit, row(aiw_ref, h) * bj_ref[h], jnp.zeros((), BF16))
            w = wh if w is None else w + wh
        return w * act

    def accumulate(vt_sub, coef, lo, hi):
        acc_ref[:, lo:hi] += _dot(vt_sub, coef[:, lo:hi])

    s_cur = s_first_ref[...]
    coef_prev = coef_last_ref[...]
    for k in range(nsb):
        vt_prev = vt_refs[k][...]
        tasks = [(kind, lo, hi) for lo, hi in halves for kind in ("pre", "acc")]
        s_parts, coef = [], []
        for ii in range(nib):
            coef.append(coefficients(k, ii, s_cur))
            for kind, lo, hi in tasks[ii * len(tasks) // nib:(ii + 1) * len(tasks) // nib]:
                if kind == "pre":
                    s_parts.append(pre_acts(k + 1, lo, hi))
                else:
                    accumulate(vt_prev, coef_prev, lo, hi)
        s_cur = jnp.concatenate(s_parts, axis=1)
        coef_prev = jnp.concatenate(coef, axis=0)
    s_first_ref[...] = s_cur
    coef_last_ref[...] = coef_prev

    @pl.when(e == ne - 1)
    def _():
        accumulate(vt_last_ref[...], coef_last_ref[...], 0, tb)
        o_ref[...] = acc_ref[...].T


def _peer(h2, wqt, keys, u, vt):
    n, d = h2.shape
    tb = min(n, 512)
    hs = (PEER_HEADS, N_KEYS, tb)
    ne, nsb = N_EXPERTS // PEER_EB, PEER_EB // PEER_SB
    last = ne * nsb - 1
    u_spec = lambda j: pl.BlockSpec((PEER_SB, d), lambda i, e: (jnp.minimum(e * nsb + j, last), 0))
    vt_spec = lambda j: pl.BlockSpec((None, d, PEER_SB), lambda i, e: (jnp.maximum(e * nsb + j, 0), 0, 0))
    return pl.pallas_call(
        _peer_kernel,
        grid=(n // tb, ne),
        in_specs=[pl.BlockSpec((tb, d), lambda i, e: (i, 0)),
                  pl.BlockSpec(wqt.shape, lambda i, e: (0, 0)),
                  pl.BlockSpec(keys.shape, lambda i, e: (0, 0, 0)),
                  pl.BlockSpec((PEER_SB, d), lambda i, e: (0, 0))]
                 + [u_spec(j) for j in range(1, nsb + 1)]
                 + [vt_spec(j) for j in range(-1, nsb - 1)]
                 + [pl.BlockSpec((None, d, PEER_SB), lambda i, e: (last, 0, 0))],
        out_specs=pl.BlockSpec((tb, d), lambda i, e: (i, 0)),
        out_shape=jax.ShapeDtypeStruct((n, d), F32),
        scratch_shapes=[pltpu.VMEM((d, tb), BF16), pltpu.VMEM(hs, BF16), pltpu.VMEM(hs, jnp.uint32),
                        pltpu.VMEM(hs, jnp.uint32), pltpu.VMEM(hs, BF16), pltpu.VMEM((d, tb), F32),
                        pltpu.VMEM((PEER_SB, tb), F32), pltpu.VMEM((PEER_SB, tb), BF16),
                        pltpu.VMEM((2, PEER_TOPK + 1, SUBLANE, tb), F32)],
        compiler_params=_cparams(("arbitrary", "arbitrary")),
        name="peer",
    )(h2, wqt, keys, u, *([u] * nsb), *([vt] * nsb), vt)


def _final_kernel(x2_ref, p_ref, mod_ref, g_ref, o_ref, *, normalize):
    x = x2_ref[...] + mod_ref[...][5:6] * p_ref[...]
    if normalize:
        x = x * lax.rsqrt(jnp.mean(x * x, axis=-1, keepdims=True) + EPS) * g_ref[...]
    o_ref[...] = x


def _final(x2, peer_out, mod, g, normalize):
    b, t, d = x2.shape
    tm = min(t, 1024)
    spec = pl.BlockSpec((None, tm, d), lambda bi, i: (bi, i, 0))
    return pl.pallas_call(
        functools.partial(_final_kernel, normalize=normalize),
        grid=(b, t // tm),
        in_specs=[spec, spec,
                  pl.BlockSpec((None, 6, d), lambda bi, i: (bi, 0, 0)),
                  pl.BlockSpec((1, d), lambda bi, i: (0, 0))],
        out_specs=spec,
        out_shape=jax.ShapeDtypeStruct((b, t, d), F32),
        compiler_params=_cparams(("arbitrary", "arbitrary")),
        name="final_norm",
    )(x2, peer_out, mod, g.reshape(1, d))


def _prep_layer_weights(w_in, conv_w, conv_b, dt_bias, a_log, d_skip, ssm_norm_g, w_branch_ssm,
                        w_branch_attn, w_out, peer_wq, peer_keys, peer_u, peer_v):
    d = w_in.shape[0]
    off_xbc = D_INNER
    off_dt = off_xbc + CONV_DIM
    off_q = off_dt + SSM_HEADS
    off_k = off_q + ATTN_DIM
    off_v = off_k + KV_DIM
    off_gate = off_v + KV_DIM
    w_re = jnp.concatenate([
        w_in[:, off_xbc:off_dt], w_in[:, :off_xbc], w_in[:, off_gate:], w_in[:, off_q:off_k],
        w_in[:, off_k:off_v], w_in[:, off_v:off_gate]], axis=1).astype(BF16)
    w_dt = jnp.pad(w_in[:, off_dt:off_q], ((0, 0), (0, DT_PAD - SSM_HEADS))).astype(BF16)
    head_of_lane = jnp.arange(D_INNER) // SSM_HEADDIM
    e1 = (jnp.arange(SSM_HEADS)[:, None] == head_of_lane[None, :]).astype(BF16)
    return dict(
        w_re=w_re, w_dt=w_dt, conv_w=conv_w, conv_b=conv_b.reshape(1, CONV_DIM),
        dtb=dt_bias.reshape(1, SSM_HEADS), alog=a_log.reshape(1, SSM_HEADS),
        dsk_x=jnp.repeat(d_skip, SSM_HEADDIM).reshape(1, D_INNER),
        norm_g=ssm_norm_g.reshape(1, D_INNER), e3=jnp.concatenate([e1, e1, e1], axis=0),
        wbs=w_branch_ssm.astype(BF16), wba=w_branch_attn.astype(BF16), wo=w_out.astype(BF16),
        wqt=peer_wq.T.astype(BF16),
        keys=peer_keys.reshape(PEER_HEADS * 2, N_KEYS, D_HALF).astype(BF16),
        u=peer_u.astype(BF16),
        vt=jnp.swapaxes(peer_v.astype(BF16).reshape(N_EXPERTS // PEER_SB, PEER_SB, -1), 1, 2))


def _layer(x, mod, conv_hist, ssm0, k_hist, v_hist, norm1_g, norm2_g, sinks, lw):
    b, t, d = x.shape
    proj, dt_raw = _proj(x, mod, norm1_g, lw["w_re"], lw["w_dt"])
    hist8 = jnp.pad(conv_hist, ((0, 0), (SUBLANE - (CONV_W - 1), 0), (0, 0)))
    h0 = ssm0.reshape(b, SSM_GROUPS, GROUP_W, D_STATE)
    y_ssm, h_new = _ssd(proj, dt_raw, hist8, h0, lw["conv_w"], lw["conv_b"], lw["dtb"], lw["alog"],
                        lw["dsk_x"], lw["norm_g"], lw["e3"])
    if k_hist is not None:
        k_hist = k_hist.reshape(b, -1, KV_DIM)
        v_hist = v_hist.reshape(b, -1, KV_DIM)
    o_attn = _attn(proj, k_hist, v_hist, sinks)
    x2, h2 = _merge(x, y_ssm, o_attn, proj, mod, norm2_g, lw["wbs"], lw["wba"], lw["wo"])
    peer_out = _peer(h2.reshape(b * t, d), lw["wqt"], lw["keys"], lw["u"], lw["vt"]).reshape(b, t, d)
    keep = WINDOW if k_hist is None else t
    k_new = proj[:, t - keep:, COL_K:COL_K + KV_DIM].astype(F32).reshape(b, keep, N_KV, HEAD_DIM)
    v_new = proj[:, t - keep:, COL_V:COL_V + KV_DIM].astype(F32).reshape(b, keep, N_KV, HEAD_DIM)
    conv_new = proj[:, t - (CONV_W - 1):, COL_XBC:COL_XBC + CONV_DIM].astype(F32)
    ssm_new = h_new.reshape(b, SSM_HEADS, SSM_HEADDIM, D_STATE)
    return x2, peer_out, k_new, v_new, conv_new, ssm_new


def kernel(x_prompt, x_sample, cache_attn_k, cache_attn_v, state_conv, state_ssm, c_prompt, c_sample, ada_w, ada_b, norm1_g, w_in, conv_w, conv_b, dt_bias, a_log, d_skip, ssm_norm_g, attn_sinks, w_branch_ssm, w_branch_attn, w_out, norm2_g, peer_wq, peer_keys, peer_u, peer_v, final_g):
    depth = ada_w.shape[0]
    d = x_prompt.shape[-1]
    bp, bs = x_prompt.shape[0], x_sample.shape[0]
    xp, xs = x_prompt, x_sample
    c_all = jnp.concatenate([c_prompt, c_sample], axis=0)
    outs = [[] for _ in range(8)]
    for l in range(depth):
        lw = _prep_layer_weights(w_in[l], conv_w[l], conv_b[l], dt_bias[l], a_log[l], d_skip[l],
                                 ssm_norm_g[l], w_branch_ssm[l], w_branch_attn[l], w_out[l],
                                 peer_wq[l], peer_keys[l], peer_u[l], peer_v[l])
        mod = _adaln(c_all, ada_w[l], ada_b[l]).reshape(bp + bs, 6, d)
        last = l == depth - 1
        conv0 = jnp.zeros((bp, CONV_W - 1, CONV_DIM), F32)
        ssm0 = jnp.zeros((bp, SSM_HEADS, SSM_HEADDIM, D_STATE), F32)
        res = []
        for x, m, ch, s0, kh, vh in ((xp, mod[:bp], conv0, ssm0, None, None),
                                     (xs, mod[bp:], state_conv[l], state_ssm[l],
                                      cache_attn_k[l], cache_attn_v[l])):
            x2, peer_out, kn, vn, cn, sn = _layer(x, m, ch, s0, kh, vh, norm1_g[l], norm2_g[l],
                                                  attn_sinks[l], lw)
            res.append((_final(x2, peer_out, m, final_g, last), kn, vn, cn, sn))
        (xp, kp, vp, cp, sp), (xs, kn, vn, cn, sn) = res
        for lst, val in zip(outs, (kp, vp, cp, sp, kn, vn, cn, sn)):
            lst.append(val)
    return (xp, xs) + tuple(jnp.stack(o) for o in outs)
```
